```python
import math
import jax, jax.numpy as jnp
from jax import lax
import numpy as np

D_MODEL = 1024
BATCH = 8
SEQ = 4096
DEPTH = 2

N_MIXERS = 2
EPS = 1e-6
ML_HEADS = 8
ML_DQK = D_MODEL // (2 * ML_HEADS)
ML_DV = D_MODEL // ML_HEADS
ML_CHUNK = 64
ML_FGATE_BIAS_LO = 3.0
ML_FGATE_BIAS_HI = 6.0
ML_M_INIT = -1e30
DA_HEADS = 8
DA_DHEAD = D_MODEL // (2 * DA_HEADS)
DA_DV = 2 * DA_DHEAD
ROPE_THETA = 500000.0
ROPE_DIM = DA_DHEAD // 4
Q_BLOCK = 128
PEER_HEADS = 8
N_KEYS = 128
N_EXPERTS = N_KEYS * N_KEYS
PEER_TOPK = 16
PEER_DQ = 256
PEER_DHALF = PEER_DQ // 2
PEER_CHUNK = 128

ML_IN_COLS = 2 * ML_HEADS * ML_DQK + 2 * ML_HEADS * ML_DV + 4 * ML_HEADS
DA_IN_COLS = 2 * DA_HEADS * DA_DHEAD * 2 + DA_HEADS * DA_DV

kernel_name = "hybrid_mlstm_diffattn_peer"


def rmsnorm(x, g):
    xf = x.astype(jnp.float32)
    y = xf * lax.rsqrt(jnp.mean(xf * xf, axis=-1, keepdims=True) + EPS)
    return (y * g.astype(jnp.float32)).astype(x.dtype)


def partial_rope(x, positions):
    half = ROPE_DIM // 2
    inv = ROPE_THETA ** (-jnp.arange(half, dtype=jnp.float32) * 2.0 / ROPE_DIM)
    ang = positions.astype(jnp.float32)[..., None] * inv
    cos = jnp.cos(ang)[:, :, None, None, :]
    sin = jnp.sin(ang)[:, :, None, None, :]
    xf = x.astype(jnp.float32)
    x1, x2, rest = xf[..., :half], xf[..., half:ROPE_DIM], xf[..., ROPE_DIM:]
    out = jnp.concatenate([x1 * cos - x2 * sin, x2 * cos + x1 * sin, rest], axis=-1)
    return out.astype(x.dtype)


def mlstm_chunkwise(q, k, v, log_i, log_f):
    B, H, S, dk = q.shape
    dv = v.shape[-1]
    L = ML_CHUNK
    nc = S // L
    q = q.reshape(B, H, nc, L, dk) * (dk ** -0.5)
    k = k.reshape(B, H, nc, L, dk)
    v = v.reshape(B, H, nc, L, dv)
    li = log_i.reshape(B, H, nc, L)
    lf = log_f.reshape(B, H, nc, L)
    b = jnp.cumsum(lf, axis=-1)
    bL = b[..., -1]
    a = bL[..., None] - b + li
    m_loc = jnp.max(a, axis=-1)
    w = jnp.exp(a - m_loc[..., None])
    C_loc = jnp.einsum('bhcs,bhcsv,bhcsk->bhcvk', w, v, k)
    n_loc = jnp.einsum('bhcs,bhcsk->bhck', w, k)

    def step(carry, xs):
        C, n, m = carry
        bL_c, m_loc_c, C_loc_c, n_loc_c = xs
        m_new = jnp.maximum(bL_c + m, m_loc_c)
        sp = jnp.exp(bL_c + m - m_new)
        sl = jnp.exp(m_loc_c - m_new)
        C_new = sp[..., None, None] * C + sl[..., None, None] * C_loc_c
        n_new = sp[..., None] * n + sl[..., None] * n_loc_c
        return (C_new, n_new, m_new), (C, n, m)

    init = (jnp.zeros((B, H, dv, dk), jnp.float32),
            jnp.zeros((B, H, dk), jnp.float32),
            jnp.full((B, H), ML_M_INIT, jnp.float32))
    xs = (jnp.moveaxis(bL, 2, 0), jnp.moveaxis(m_loc, 2, 0),
          jnp.moveaxis(C_loc, 2, 0), jnp.moveaxis(n_loc, 2, 0))
    _, (C_prev, n_prev, m_prev) = lax.scan(step, init, xs)
    C_prev = jnp.moveaxis(C_prev, 0, 2)
    n_prev = jnp.moveaxis(n_prev, 0, 2)
    m_prev = jnp.moveaxis(m_prev, 0, 2)

    mask = jnp.tril(jnp.ones((L, L), dtype=bool))
    D = b[..., :, None] - b[..., None, :] + li[..., None, :]
    D = jnp.where(mask, D, -jnp.inf)
    g = b + m_prev[..., None]
    m_t = jnp.maximum(g, jnp.max(D, axis=-1))
    Sqk = jnp.einsum('bhctd,bhcsd->bhcts', q, k) * jnp.exp(D - m_t[..., None])
    inter = jnp.exp(g - m_t)
    num = (jnp.einsum('bhcts,bhcsv->bhctv', Sqk, v)
           + inter[..., None] * jnp.einsum('bhcvk,bhctk->bhctv', C_prev, q))
    den = jnp.sum(Sqk, axis=-1) + inter * jnp.einsum('bhck,bhctk->bhct', n_prev, q)
    h = num / jnp.maximum(jnp.abs(den), jnp.exp(-m_t))[..., None]
    return h.reshape(B, H, S, dv)


def mlstm_mixer(x, norm_g, w_in, b_gates, head_g, w_out):
    B, S, _ = x.shape
    h = rmsnorm(x, norm_g)
    proj = h @ w_in
    nq = ML_HEADS * ML_DQK
    nv = ML_HEADS * ML_DV
    q, k, v, o, gates = jnp.split(proj, [nq, 2 * nq, 2 * nq + nv, 2 * nq + 2 * nv], axis=-1)

    def heads(t, d):
        return t.reshape(B, S, ML_HEADS, d).transpose(0, 2, 1, 3).astype(jnp.float32)

    q, k, v = heads(q, ML_DQK), heads(k, ML_DQK), heads(v, ML_DV)
    gates = gates.astype(jnp.float32) + b_gates.astype(jnp.float32)
    gates = gates.reshape(B, S, 4, ML_HEADS).transpose(2, 0, 3, 1)
    i_fwd, f_fwd, i_bwd, f_bwd = gates[0], gates[1], gates[2], gates[3]
    h_fwd = mlstm_chunkwise(q, k, v, i_fwd, jax.nn.log_sigmoid(f_fwd))
    flip = lambda t: jnp.flip(t, axis=2)
    h_bwd = flip(mlstm_chunkwise(flip(q), flip(k), flip(v), flip(i_bwd),
                                 flip(jax.nn.log_sigmoid(f_bwd))))
    hs = (h_fwd + h_bwd).transpose(0, 2, 1, 3)
    hs = hs * lax.rsqrt(jnp.mean(hs * hs, axis=-1, keepdims=True) + EPS)
    hs = hs.reshape(B, S, D_MODEL) * head_g.astype(jnp.float32)
    hs = hs * jax.nn.sigmoid(o.astype(jnp.float32))
    return hs.astype(x.dtype) @ w_out


def diff_attn_mixer(x, positions, norm_g, w_qkv, q_g, k_g, lam_q1, lam_k1, lam_q2, lam_k2,
                    subln_g, w_out, lambda_init):
    B, S, _ = x.shape
    h = rmsnorm(x, norm_g)
    q, k, v = jnp.split(h @ w_qkv, 3, axis=-1)
    q = q.reshape(B, S, DA_HEADS, 2, DA_DHEAD)
    k = k.reshape(B, S, DA_HEADS, 2, DA_DHEAD)
    v = v.reshape(B, S, DA_HEADS, DA_DV).transpose(0, 2, 1, 3)
    q = partial_rope(rmsnorm(q, q_g), positions).transpose(0, 2, 3, 1, 4)
    k = partial_rope(rmsnorm(k, k_g), positions).transpose(0, 2, 3, 1, 4)
    lam = (jnp.exp(jnp.sum(lam_q1.astype(jnp.float32) * lam_k1.astype(jnp.float32)))
           - jnp.exp(jnp.sum(lam_q2.astype(jnp.float32) * lam_k2.astype(jnp.float32)))
           + lambda_init)
    scale = DA_DHEAD ** -0.5
    nb = S // Q_BLOCK
    qb = jnp.moveaxis(q.reshape(B, DA_HEADS, 2, nb, Q_BLOCK, DA_DHEAD), 3, 0)

    def attend(qblk):
        s = jnp.einsum('bhcqd,bhckd->bhcqk', qblk, k).astype(jnp.float32) * scale
        p = jax.nn.softmax(s, axis=-1)
        a = p[:, :, 0] - lam * p[:, :, 1]
        return jnp.einsum('bhqk,bhkv->bhqv', a.astype(v.dtype), v)

    o = lax.map(attend, qb)
    o = o.transpose(1, 0, 3, 2, 4).reshape(B, S, DA_HEADS, DA_DV)
    o = rmsnorm(o, subln_g) * (1.0 - lambda_init)
    return o.reshape(B, S, D_MODEL) @ w_out


def peer_ffn(x, norm_g, w_query, sub_keys, u_table, v_table):
    B, S, D = x.shape
    T = B * S
    h = rmsnorm(x, norm_g).reshape(T, D)
    q = (h @ w_query).reshape(T, PEER_HEADS, 2, PEER_DHALF)
    s = jnp.einsum('thpd,hpnd->thpn', q, sub_keys).astype(jnp.float32)
    vals, idx = lax.top_k(s, PEER_TOPK)
    cand = vals[:, :, 0, :, None] + vals[:, :, 1, None, :]
    cidx = idx[:, :, 0, :, None] * N_KEYS + idx[:, :, 1, None, :]
    cand = cand.reshape(T, PEER_HEADS, PEER_TOPK * PEER_TOPK)
    cidx = cidx.reshape(T, PEER_HEADS, PEER_TOPK * PEER_TOPK)
    top_s, pos = lax.top_k(cand, PEER_TOPK)
    experts = jnp.take_along_axis(cidx, pos, axis=-1)
    gates = jax.nn.softmax(top_s, axis=-1)
    nch = T // PEER_CHUNK
    E = PEER_HEADS * PEER_TOPK
    hc = h.reshape(nch, PEER_CHUNK, D)
    ec = experts.reshape(nch, PEER_CHUNK, E)
    gc = gates.reshape(nch, PEER_CHUNK, E)

    def block(args):
        hb, eb, gb = args
        ue = u_table[eb]
        act = jax.nn.gelu(jnp.einsum('td,ted->te', hb, ue), approximate=False)
        ve = v_table[eb]
        return jnp.einsum('te,ted->td', (gb * act.astype(jnp.float32)).astype(ve.dtype), ve)

    out = lax.map(block, (hc, ec, gc))
    return out.reshape(B, S, D).astype(x.dtype)


def setup_inputs(seed: int = 0) -> dict:
    key = jax.random.key(seed)
    ks = jax.random.split(key, 40)
    f32 = jnp.float32
    nrm = lambda k, shape, sc: jax.random.normal(k, shape, f32) * sc
    gain = lambda k, n: 1.0 + 0.02 * jax.random.normal(k, (n,), f32)
    fan = D_MODEL ** -0.5
    x = jax.random.normal(ks[0], (BATCH, SEQ, D_MODEL), f32)
    positions = (jnp.arange(SEQ, dtype=jnp.int32)[None, :]
                 + jax.random.randint(ks[1], (BATCH, 1), 0, 4096, dtype=jnp.int32))
    b_i = 0.1 * jax.random.normal(ks[4], (ML_HEADS,), f32)
    b_f = jnp.linspace(ML_FGATE_BIAS_LO, ML_FGATE_BIAS_HI, ML_HEADS, dtype=f32) + 0.1 * jax.random.normal(ks[5], (ML_HEADS,), f32)
    b_ib = 0.1 * jax.random.normal(ks[6], (ML_HEADS,), f32)
    b_fb = jnp.linspace(ML_FGATE_BIAS_LO, ML_FGATE_BIAS_HI, ML_HEADS, dtype=f32) + 0.1 * jax.random.normal(ks[7], (ML_HEADS,), f32)
    return {
        "x": x,
        "positions": positions,
        "l0_norm_mix": gain(ks[2], D_MODEL),
        "l0_w_in": nrm(ks[3], (D_MODEL, ML_IN_COLS), fan),
        "l0_b_gates": jnp.concatenate([b_i, b_f, b_ib, b_fb]),
        "l0_head_norm": gain(ks[8], D_MODEL),
        "l0_w_out": nrm(ks[9], (D_MODEL, D_MODEL), fan),
        "l0_norm_ffn": gain(ks[10], D_MODEL),
        "l0_peer_wq": nrm(ks[11], (D_MODEL, PEER_HEADS * PEER_DQ), fan),
        "l0_peer_keys": nrm(ks[12], (PEER_HEADS, 2, N_KEYS, PEER_DHALF), PEER_DHALF ** -0.5),
        "l0_peer_u": nrm(ks[13], (N_EXPERTS, D_MODEL), fan),
        "l0_peer_v": nrm(ks[14], (N_EXPERTS, D_MODEL), 0.25),
        "l1_norm_mix": gain(ks[15], D_MODEL),
        "l1_w_qkv": nrm(ks[16], (D_MODEL, DA_IN_COLS), fan),
        "l1_q_norm": gain(ks[17], DA_DHEAD),
        "l1_k_norm": gain(ks[18], DA_DHEAD),
        "l1_lambda_q1": nrm(ks[19], (DA_DHEAD,), 0.1),
        "l1_lambda_k1": nrm(ks[20], (DA_DHEAD,), 0.1),
        "l1_lambda_q2": nrm(ks[21], (DA_DHEAD,), 0.1),
        "l1_lambda_k2": nrm(ks[22], (DA_DHEAD,), 0.1),
        "l1_subln": gain(ks[23], DA_DV),
        "l1_w_out": nrm(ks[24], (D_MODEL, D_MODEL), fan),
        "l1_norm_ffn": gain(ks[25], D_MODEL),
        "l1_peer_wq": nrm(ks[26], (D_MODEL, PEER_HEADS * PEER_DQ), fan),
        "l1_peer_keys": nrm(ks[27], (PEER_HEADS, 2, N_KEYS, PEER_DHALF), PEER_DHALF ** -0.5),
        "l1_peer_u": nrm(ks[28], (N_EXPERTS, D_MODEL), fan),
        "l1_peer_v": nrm(ks[29], (N_EXPERTS, D_MODEL), 0.25),
    }


def reference(x, positions,
              l0_norm_mix, l0_w_in, l0_b_gates, l0_head_norm, l0_w_out,
              l0_norm_ffn, l0_peer_wq, l0_peer_keys, l0_peer_u, l0_peer_v,
              l1_norm_mix, l1_w_qkv, l1_q_norm, l1_k_norm,
              l1_lambda_q1, l1_lambda_k1, l1_lambda_q2, l1_lambda_k2, l1_subln, l1_w_out,
              l1_norm_ffn, l1_peer_wq, l1_peer_keys, l1_peer_u, l1_peer_v):
    mix_params = [
        (l0_norm_mix, l0_w_in, l0_b_gates, l0_head_norm, l0_w_out),
        (l1_norm_mix, l1_w_qkv, l1_q_norm, l1_k_norm, l1_lambda_q1, l1_lambda_k1,
         l1_lambda_q2, l1_lambda_k2, l1_subln, l1_w_out),
    ]
    ffn_params = [
        (l0_norm_ffn, l0_peer_wq, l0_peer_keys, l0_peer_u, l0_peer_v),
        (l1_norm_ffn, l1_peer_wq, l1_peer_keys, l1_peer_u, l1_peer_v),
    ]
    for i in range(DEPTH):
        if i % N_MIXERS == 0:
            x = x + mlstm_mixer(x, *mix_params[i])
        else:
            lambda_init = 0.8 - 0.6 * math.exp(-0.3 * i)
            x = x + diff_attn_mixer(x, positions, *mix_params[i], lambda_init=lambda_init)
        x = x + peer_ffn(x, *ffn_params[i])
    return x
```

```python
import functools
import math

import jax
import jax.numpy as jnp
from jax import lax
from jax.experimental import pallas as pl
from jax.experimental.pallas import tpu as pltpu

F32 = jnp.float32
BF16 = jnp.bfloat16
HIGHEST = lax.Precision.HIGHEST

D_MODEL = 1024
EPS = 1e-6
LANES = 128
N_HEADS = 8
ML_DQK = 64
ML_CHUNK = 128
ML_M_INIT = -1e30
DA_DHEAD = 64
ROPE_THETA = 500000.0
ROPE_HALF = 8
N_KEYS = 128
N_EXPERTS = N_KEYS * N_KEYS
PEER_TOPK = 16
VMEM_LIMIT_BYTES = 56 * 1024 * 1024


def _cparams(sem):
    return pltpu.CompilerParams(dimension_semantics=sem, vmem_limit_bytes=VMEM_LIMIT_BYTES)


def _rms_rows(x, g):
    ms = jnp.mean(x * x, axis=-1, keepdims=True)
    return x * lax.rsqrt(ms + EPS) * g


def _norm_proj_kernel(x_ref, g_ref, w_ref, *out_refs, segs):
    hn = _rms_rows(x_ref[...], g_ref[...]).astype(BF16)
    for o_ref, (start, width) in zip(out_refs, segs):
        o_ref[...] = jnp.dot(hn, w_ref[:, start:start + width],
                             preferred_element_type=F32).astype(o_ref.dtype)


def _norm_proj(x2, g, w_cat, segs, dtypes, tm):
    t = x2.shape[0]
    n = w_cat.shape[1]
    return pl.pallas_call(
        functools.partial(_norm_proj_kernel, segs=tuple(segs)),
        grid=(t // tm,),
        in_specs=[pl.BlockSpec((tm, D_MODEL), lambda i: (i, 0)),
                  pl.BlockSpec((1, D_MODEL), lambda i: (0, 0)),
                  pl.BlockSpec((D_MODEL, n), lambda i: (0, 0))],
        out_specs=[pl.BlockSpec((tm, w), lambda i: (i, 0)) for (_, w) in segs],
        out_shape=[jax.ShapeDtypeStruct((t, w), dt) for (_, w), dt in zip(segs, dtypes)],
        compiler_params=_cparams(("parallel",)),
        name="norm_proj",
    )(x2, g.reshape(1, D_MODEL), w_cat)


def _mlstm_kernel(q_ref, k_ref, v_ref, g_ref, bias_ref, m_ref, mt_ref, o_ref, ct_sc, ms_sc, *, L):
    c = pl.program_id(2)

    @pl.when(c == 0)
    def _():
        ct_sc[...] = jnp.zeros_like(ct_sc)
        ms_sc[...] = jnp.full_like(ms_sc, ML_M_INIT)

    g = g_ref[...] + bias_ref[...]
    lf = jnp.minimum(g, 0.0) - jnp.log1p(jnp.exp(-jnp.abs(g)))
    msk_f = m_ref[...]
    bcol_all = jnp.dot(msk_f, lf, precision=HIGHEST, preferred_element_type=F32)
    g_t = g.T
    brow_all = jnp.dot(lf.T, mt_ref[...], precision=HIGHEST, preferred_element_type=F32)
    allowed = msk_f > 0.5
    lane = lax.broadcasted_iota(jnp.int32, (L, LANES), 1)
    ones_blk = (lane == 0).astype(BF16)

    for h in range(N_HEADS):
        sl = slice(h * LANES, (h + 1) * LANES)
        q = q_ref[:, sl] * (ML_DQK ** -0.5)
        k = k_ref[:, sl]
        v_aug = jnp.concatenate([v_ref[:, sl], ones_blk], axis=1)
        b_col = bcol_all[:, N_HEADS + h:N_HEADS + h + 1]
        b_row = brow_all[N_HEADS + h:N_HEADS + h + 1, :]
        li_col = g[:, h:h + 1]
        li_row = g_t[h:h + 1, :]
        b_tot = jnp.min(b_row, axis=1, keepdims=True)
        m_prev = ms_sc[h]
        ct_prev = ct_sc[h]

        a_col = b_tot - b_col + li_col
        m_loc = jnp.max(a_col, axis=0, keepdims=True)
        w_col = jnp.exp(a_col - m_loc)

        dmat = jnp.where(allowed, b_col - b_row + li_row, -jnp.inf)
        g_col = b_col + m_prev
        m_t = jnp.maximum(g_col, jnp.max(dmat, axis=1, keepdims=True))
        sqk = lax.dot_general(q, k, (((1,), (1,)), ((), ())), preferred_element_type=F32)
        sqk = sqk * jnp.exp(dmat - m_t)
        inter = jnp.exp(g_col - m_t)
        nd = (jnp.dot(sqk.astype(BF16), v_aug, preferred_element_type=F32)
              + inter * jnp.dot(q, ct_prev.astype(BF16), preferred_element_type=F32))
        den = jnp.maximum(jnp.abs(nd[:, LANES:LANES + 1]), jnp.exp(-m_t))
        o_ref[:, sl] = nd[:, :LANES] / den

        m_new = jnp.maximum(b_tot + m_prev, m_loc)
        sp = jnp.exp(b_tot + m_prev - m_new)
        sc = jnp.exp(m_loc - m_new)
        wv = (w_col * v_aug.astype(F32)).astype(BF16)
        ct_loc = lax.dot_general(k, wv, (((0,), (0,)), ((), ())), preferred_element_type=F32)
        ct_sc[h] = sp * ct_prev + sc * ct_loc
        ms_sc[h] = m_new


def _mlstm(q, k, v, gates, bias, b, s):
    L = min(ML_CHUNK, s)
    nc = s // L
    tri = jnp.tril(jnp.ones((L, L), F32))
    masks = jnp.stack([tri, tri.T])
    masks_t = jnp.stack([tri.T, tri])

    def cidx(d, c):
        return c + d * (nc - 1 - 2 * c)

    seq_spec = pl.BlockSpec((None, L, D_MODEL), lambda bi, d, c: (bi, cidx(d, c), 0))
    return pl.pallas_call(
        functools.partial(_mlstm_kernel, L=L),
        grid=(b, 2, nc),
        in_specs=[seq_spec, seq_spec, seq_spec,
                  pl.BlockSpec((None, None, L, LANES), lambda bi, d, c: (d, bi, cidx(d, c), 0)),
                  pl.BlockSpec((None, 1, LANES), lambda bi, d, c: (d, 0, 0)),
                  pl.BlockSpec((None, L, L), lambda bi, d, c: (d, 0, 0)),
                  pl.BlockSpec((None, L, L), lambda bi, d, c: (d, 0, 0))],
        out_specs=pl.BlockSpec((None, None, L, D_MODEL), lambda bi, d, c: (d, bi, cidx(d, c), 0)),
        out_shape=jax.ShapeDtypeStruct((2, b, s, D_MODEL), F32),
        scratch_shapes=[pltpu.VMEM((N_HEADS, LANES, 2 * LANES), F32),
                        pltpu.VMEM((N_HEADS, 1, 1), F32)],
        compiler_params=_cparams(("parallel", "parallel", "arbitrary")),
        name="mlstm_scan",
    )(q, k, v, gates, bias, masks, masks_t)


def _mlstm_out_kernel(x_ref, h_ref, o_ref, hg_ref, w_ref, out_ref):
    hs = h_ref[0] + h_ref[1]
    parts = []
    for h in range(N_HEADS):
        blk = hs[:, h * LANES:(h + 1) * LANES]
        ms = jnp.mean(blk * blk, axis=-1, keepdims=True)
        parts.append(blk * lax.rsqrt(ms + EPS))
    hn = jnp.concatenate(parts, axis=1) * hg_ref[...] * jax.nn.sigmoid(o_ref[...])
    out_ref[...] = x_ref[...] + jnp.dot(hn.astype(BF16), w_ref[...], preferred_element_type=F32)


def _mlstm_out(x2, h2, o, head_g, w_out, tm):
    t = x2.shape[0]
    return pl.pallas_call(
        _mlstm_out_kernel,
        grid=(t // tm,),
        in_specs=[pl.BlockSpec((tm, D_MODEL), lambda i: (i, 0)),
                  pl.BlockSpec((2, tm, D_MODEL), lambda i: (0, i, 0)),
                  pl.BlockSpec((tm, D_MODEL), lambda i: (i, 0)),
                  pl.BlockSpec((1, D_MODEL), lambda i: (0, 0)),
                  pl.BlockSpec((D_MODEL, D_MODEL), lambda i: (0, 0))],
        out_specs=pl.BlockSpec((tm, D_MODEL), lambda i: (i, 0)),
        out_shape=jax.ShapeDtypeStruct((t, D_MODEL), F32),
        compiler_params=_cparams(("parallel",)),
        name="mlstm_out",
    )(x2, h2, o, head_g.reshape(1, D_MODEL), w_out)


def _da_proj_kernel(x_ref, g_ref, w_ref, pos_ref, inv_ref, wts_ref, gq_ref, gk_ref,
                    cq_ref, ck_ref, q_ref, k_ref, v_ref):
    hn = _rms_rows(x_ref[...], g_ref[...]).astype(BF16)
    ang = pos_ref[...] * inv_ref[...]
    cos = jnp.cos(ang)
    sin = jnp.sin(ang)
    trig_q = cq_ref[0:1, :] * cos + cq_ref[1:2, :] * sin + cq_ref[2:3, :]
    trig_k = ck_ref[0:1, :] * cos + ck_ref[1:2, :] * sin + ck_ref[2:3, :]
    fq = trig_q * gq_ref[...]
    fk = trig_k * gk_ref[...]
    nsub = 2 * N_HEADS
    for u in range(nsub):
        for off, fac, wts, o_ref in ((0, fq, wts_ref[0:1, :], q_ref), (nsub * LANES, fk, wts_ref[1:2, :], k_ref)):
            raw = jnp.dot(hn, w_ref[:, off + u * LANES: off + (u + 1) * LANES], preferred_element_type=F32)
            ms = jnp.sum(raw * raw * wts, axis=-1, keepdims=True) * (1.0 / DA_DHEAD)
            o_ref[:, u * LANES:(u + 1) * LANES] = (raw * lax.rsqrt(ms + EPS) * fac).astype(BF16)
    v_ref[...] = jnp.dot(hn, w_ref[:, 2 * nsub * LANES:], preferred_element_type=F32).astype(BF16)


def _da_layout(w_qkv, q_g, k_g):
    x1 = list(range(0, 8))
    x2 = list(range(8, 16))
    rest = list(range(16, 64))
    q_groups = [(x1, 1, 0, 0), (x1, 1, 0, 0), (x2, 0, -1, 0), (x2, 0, -1, 0),
                (x2, 1, 0, 0), (x2, 1, 0, 0), (x1, 0, 1, 0), (x1, 0, 1, 0)]
    k_groups = [(x1, 1, 0, 0), (x2, 0, -1, 0), (x1, 1, 0, 0), (x2, 0, -1, 0),
                (x2, 1, 0, 0), (x1, 0, 1, 0), (x2, 1, 0, 0), (x1, 0, 1, 0)]

    def tables(groups):
        src, coef = [], []
        for dims, a, b_, c_ in groups:
            src += dims
            coef += [(a, b_, c_)] * len(dims)
        src += rest
        coef += [(0, 0, 1)] * len(rest)
        pad = LANES - len(src)
        valid = [1.0] * len(src) + [0.0] * pad
        src += [0] * pad
        coef += [(0, 0, 0)] * pad
        return jnp.array(src, jnp.int32), jnp.array(coef, F32).T, jnp.array(valid, F32)

    src_q, coef_q, valid_q = tables(q_groups)
    src_k, coef_k, valid_k = tables(k_groups)
    nsub = 2 * N_HEADS
    wq = w_qkv[:, :nsub * DA_DHEAD].reshape(D_MODEL, nsub, DA_DHEAD)
    wk = w_qkv[:, nsub * DA_DHEAD:2 * nsub * DA_DHEAD].reshape(D_MODEL, nsub, DA_DHEAD)
    wv = w_qkv[:, 2 * nsub * DA_DHEAD:]
    wq_p = (wq[:, :, src_q] * valid_q).reshape(D_MODEL, nsub * LANES)
    wk_p = (wk[:, :, src_k] * valid_k).reshape(D_MODEL, nsub * LANES)
    w_cat = jnp.concatenate([wq_p, wk_p, wv], axis=1).astype(BF16)
    lanes = jnp.arange(LANES)
    plain = (lanes >= 64) & (lanes < 112)
    wts_q = ((lanes < 8) | ((lanes >= 16) & (lanes < 24)) | plain).astype(F32)
    wts_k = ((lanes < 16) | plain).astype(F32)
    wts = jnp.concatenate([wts_q[None], wts_k[None], jnp.zeros((6, LANES), F32)], axis=0)
    gq = (q_g[src_q] * valid_q).reshape(1, LANES) * (DA_DHEAD ** -0.5)
    gk = (k_g[src_k] * valid_k).reshape(1, LANES)
    freq = jnp.where(lanes < 64, lanes % ROPE_HALF, 0)
    inv = ROPE_THETA ** (-freq.astype(F32) * 2.0 / (2 * ROPE_HALF))
    inv = jnp.where(lanes < 64, inv, 0.0).reshape(1, LANES)
    pad8 = lambda a: jnp.concatenate([a, jnp.zeros((5, LANES), F32)], axis=0)
    return w_cat, inv, wts, gq, gk, pad8(coef_q), pad8(coef_k)


def _da_proj(x2, norm_g, w_qkv, q_g, k_g, pos, tm):
    t = x2.shape[0]
    w_cat, inv, wts, gq, gk, cq, ck = _da_layout(w_qkv, q_g, k_g)
    nq = 2 * N_HEADS * LANES
    const = lambda shp: pl.BlockSpec(shp, lambda i: (0,) * len(shp))
    return pl.pallas_call(
        _da_proj_kernel,
        grid=(t // tm,),
        in_specs=[pl.BlockSpec((tm, D_MODEL), lambda i: (i, 0)),
                  const((1, D_MODEL)), const(w_cat.shape),
                  pl.BlockSpec((tm, 1), lambda i: (i, 0)),
                  const((1, LANES)), const((8, LANES)), const((1, LANES)), const((1, LANES)),
                  const((8, LANES)), const((8, LANES))],
        out_specs=[pl.BlockSpec((tm, nq), lambda i: (i, 0)),
                   pl.BlockSpec((tm, nq), lambda i: (i, 0)),
                   pl.BlockSpec((tm, D_MODEL), lambda i: (i, 0))],
        out_shape=[jax.ShapeDtypeStruct((t, nq), BF16),
                   jax.ShapeDtypeStruct((t, nq), BF16),
                   jax.ShapeDtypeStruct((t, D_MODEL), BF16)],
        compiler_params=_cparams(("parallel",)),
        name="da_proj",
    )(x2, norm_g.reshape(1, D_MODEL), w_cat, pos, inv, wts, gq, gk, cq, ck)


def _da_attn_kernel(q_ref, k_ref, v_ref, lam_ref, sg_ref, o_ref, *, lambda_init):
    out_scale = 1.0 - lambda_init
    lam_v = lam_ref[...]
    lam = (jnp.exp(jnp.sum(lam_v[0:1] * lam_v[1:2], axis=-1, keepdims=True))
           - jnp.exp(jnp.sum(lam_v[2:3] * lam_v[3:4], axis=-1, keepdims=True))
           + lambda_init)
    v = v_ref[...]
    outs = []
    for c in range(2):
        q = q_ref[:, c * LANES:(c + 1) * LANES]
        k = k_ref[:, c * LANES:(c + 1) * LANES]
        s = lax.dot_general(q, k, (((1,), (1,)), ((), ())), preferred_element_type=F32)
        m = jnp.max(s, axis=-1, keepdims=True)
        p = jnp.exp(s - m)
        l = jnp.sum(p, axis=-1, keepdims=True)
        outs.append(jnp.dot(p.astype(BF16), v, preferred_element_type=F32) / l)
    o = outs[0] - lam * outs[1]
    ms = jnp.mean(o * o, axis=-1, keepdims=True)
    o_ref[...] = (o * lax.rsqrt(ms + EPS) * sg_ref[...]) * out_scale


def _da_attn(q, k, v, lam4, subln_g, b, s, lambda_init, tq):
    tq = min(tq, s)
    return pl.pallas_call(
        functools.partial(_da_attn_kernel, lambda_init=lambda_init),
        grid=(b, N_HEADS, s // tq),
        in_specs=[pl.BlockSpec((None, tq, 2 * LANES), lambda bi, h, i: (bi, i, h)),
                  pl.BlockSpec((None, s, 2 * LANES), lambda bi, h, i: (bi, 0, h)),
                  pl.BlockSpec((None, s, LANES), lambda bi, h, i: (bi, 0, h)),
                  pl.BlockSpec((4, DA_DHEAD), lambda bi, h, i: (0, 0)),
                  pl.BlockSpec((1, LANES), lambda bi, h, i: (0, 0))],
        out_specs=pl.BlockSpec((None, tq, LANES), lambda bi, h, i: (bi, i, h)),
        out_shape=jax.ShapeDtypeStruct((b, s, D_MODEL), F32),
        compiler_params=_cparams(("parallel", "parallel", "arbitrary")),
        name="da_attn",
    )(q, k, v, lam4, subln_g.reshape(1, LANES))


def _resid_matmul_kernel(x_ref, a_ref, w_ref, out_ref):
    out_ref[...] = x_ref[...] + jnp.dot(a_ref[...].astype(BF16), w_ref[...], preferred_element_type=F32)


def _resid_matmul(x2, a2, w, tm):
    t = x2.shape[0]
    return pl.pallas_call(
        _resid_matmul_kernel,
        grid=(t // tm,),
        in_specs=[pl.BlockSpec((tm, D_MODEL), lambda i: (i, 0)),
                  pl.BlockSpec((tm, D_MODEL), lambda i: (i, 0)),
                  pl.BlockSpec((D_MODEL, D_MODEL), lambda i: (0, 0))],
        out_specs=pl.BlockSpec((tm, D_MODEL), lambda i: (i, 0)),
        out_shape=jax.ShapeDtypeStruct((t, D_MODEL), F32),
        compiler_params=_cparams(("parallel",)),
        name="resid_matmul",
    )(x2, a2, w)


N_EXTRACT = PEER_TOPK + 1
A_ROWS = 24
N_FULL_ROWS = 8


def _extract_desc(vals, n, store):
    for kk in range(n):
        mx = jnp.max(vals, axis=0, keepdims=True)
        store(kk, mx)
        vals = jnp.where(vals == mx, -jnp.inf, vals)


def _peer_route_kernel(x_ref, g_ref, wq_ref, keys_ref, hnt_ref, thr_ref, e1z_ref, s2_ref, e2_ref,
                       a1_sc, a2_sc):
    tm = x_ref.shape[0]
    hn = _rms_rows(x_ref[...], g_ref[...])
    hnt_ref[...] = hn.T.astype(BF16)
    q = jnp.dot(hn.astype(BF16), wq_ref[...], preferred_element_type=F32).astype(BF16)
    neg = jnp.full((A_ROWS, tm), -jnp.inf, F32)
    row = lax.broadcasted_iota(jnp.int32, (A_ROWS, tm), 0)
    for h in range(N_HEADS):
        s_t = []
        for p, a_sc in ((0, a1_sc), (1, a2_sc)):
            hp = 2 * h + p
            st = lax.dot_general(keys_ref[hp], q[:, hp * LANES:(hp + 1) * LANES],
                                 (((1,), (1,)), ((), ())), preferred_element_type=F32)
            a_sc[...] = neg

            def store(kk, r, a_sc=a_sc):
                a_sc[kk:kk + 1, :] = r
            _extract_desc(st, N_EXTRACT, store)
            s_t.append(st)
        a1 = a1_sc[...]
        a2 = a2_sc[...]
        blocks = [a1[p:p + 1, :] + a2 for p in range(N_FULL_ROWS)]
        blocks.append(jnp.where(row >= N_FULL_ROWS, a1 + a2[0:1, :], -jnp.inf))
        cvals = []
        for kk in range(N_EXTRACT):
            cur = blocks[0]
            for blk in blocks[1:]:
                cur = jnp.maximum(cur, blk)
            mx = jnp.max(cur, axis=0, keepdims=True)
            cvals.append(mx)
            blocks = [jnp.where(blk == mx, -jnp.inf, blk) for blk in blocks]
        c0 = cvals[0]
        z = jnp.ones_like(c0)
        for kk in range(1, PEER_TOPK):
            z = z + jnp.exp(cvals[kk] - c0)
        tau = 0.5 * (cvals[PEER_TOPK - 1] + cvals[PEER_TOPK])
        s1, s2 = s_t
        thr_ref[h] = tau - s1
        e1z_ref[h] = jnp.exp(s1 - a1[0:1, :]) / z
        s2_ref[h] = s2
        e2_ref[h] = jnp.exp(s2 - a2[0:1, :])


def _peer_route(x2, norm_g, wq, keys, tm):
    t = x2.shape[0]
    tab = jax.ShapeDtypeStruct((N_HEADS, N_KEYS, t), F32)
    tab_spec = pl.BlockSpec((N_HEADS, N_KEYS, tm), lambda i: (0, 0, i))
    return pl.pallas_call(
        _peer_route_kernel,
        grid=(t // tm,),
        in_specs=[pl.BlockSpec((tm, D_MODEL), lambda i: (i, 0)),
                  pl.BlockSpec((1, D_MODEL), lambda i: (0, 0)),
                  pl.BlockSpec(wq.shape, lambda i: (0, 0)),
                  pl.BlockSpec(keys.shape, lambda i: (0, 0, 0))],
        out_specs=[pl.BlockSpec((D_MODEL, tm), lambda i: (0, i)), tab_spec, tab_spec, tab_spec, tab_spec],
        out_shape=[jax.ShapeDtypeStruct((D_MODEL, t), BF16), tab, tab, tab, tab],
        scratch_shapes=[pltpu.VMEM((A_ROWS, tm), F32), pltpu.VMEM((A_ROWS, tm), F32)],
        compiler_params=_cparams(("parallel",)),
        name="peer_route",
    )(x2, norm_g.reshape(1, D_MODEL), wq, keys)


def _peer_expert_kernel(hnt_ref, u_ref, vt_ref, thr_ref, e1z_ref, s2_ref, e2_ref, x_ref, out_ref, acc_sc,
                        *, ec):
    c = pl.program_id(1)
    tm = hnt_ref.shape[1]

    @pl.when(c == 0)
    def _():
        acc_sc[...] = jnp.zeros_like(acc_sc)

    s_t = jnp.dot(u_ref[...], hnt_ref[...], preferred_element_type=F32)
    n_i = ec // N_KEYS
    ps = []
    for ii in range(n_i):
        i = c * n_i + ii
        gate = jnp.zeros((N_KEYS, tm), F32)
        for h in range(N_HEADS):
            thr = thr_ref[h, pl.ds(i, 1), :]
            e1 = e1z_ref[h, pl.ds(i, 1), :]
            gate = gate + jnp.where(s2_ref[h] >= thr, e2_ref[h], 0.0) * e1
        s = s_t[ii * N_KEYS:(ii + 1) * N_KEYS, :]
        act = 0.5 * s * (1.0 + lax.erf(s * (2.0 ** -0.5)))
        ps.append((gate * act).astype(BF16))
    p = jnp.concatenate(ps, axis=0)
    acc_sc[...] += jnp.dot(vt_ref[...], p, preferred_element_type=F32)

    @pl.when(c == pl.num_programs(1) - 1)
    def _():
        out_ref[...] = x_ref[...] + acc_sc[...].T


def _peer_expert(x2, hnt, u, vt, thr, e1z, s2, e2, tm, ec):
    t = x2.shape[0]
    tab_spec = pl.BlockSpec((N_HEADS, N_KEYS, tm), lambda i, c: (0, 0, i))
    return pl.pallas_call(
        functools.partial(_peer_expert_kernel, ec=ec),
        grid=(t // tm, N_EXPERTS // ec),
        in_specs=[pl.BlockSpec((D_MODEL, tm), lambda i, c: (0, i)),
                  pl.BlockSpec((ec, D_MODEL), lambda i, c: (c, 0)),
                  pl.BlockSpec((D_MODEL, ec), lambda i, c: (0, c)),
                  tab_spec, tab_spec, tab_spec, tab_spec,
                  pl.BlockSpec((tm, D_MODEL), lambda i, c: (i, 0))],
        out_specs=pl.BlockSpec((tm, D_MODEL), lambda i, c: (i, 0)),
        out_shape=jax.ShapeDtypeStruct((t, D_MODEL), F32),
        scratch_shapes=[pltpu.VMEM((D_MODEL, tm), F32)],
        compiler_params=_cparams(("parallel", "arbitrary")),
        name="peer_expert",
    )(hnt, u, vt, thr, e1z, s2, e2, x2)


def _peer_ffn(x2, norm_g, w_query, sub_keys, u_table, v_table, tm_route, tm_exp, ec):
    keys = sub_keys.reshape(2 * N_HEADS, N_KEYS, LANES).astype(BF16)
    hnt, thr, e1z, s2, e2 = _peer_route(x2, norm_g, w_query.astype(BF16), keys, tm_route)
    return _peer_expert(x2, hnt, u_table.astype(BF16), v_table.T.astype(BF16), thr, e1z, s2, e2, tm_exp, ec)


def _mlstm_layout(w_in, b_gates):
    nq = N_HEADS * ML_DQK
    pad_heads = lambda w: jnp.pad(w.reshape(D_MODEL, N_HEADS, ML_DQK),
                                  ((0, 0), (0, 0), (0, LANES - ML_DQK))).reshape(D_MODEL, N_HEADS * LANES)
    wq = pad_heads(w_in[:, :nq])
    wk = pad_heads(w_in[:, nq:2 * nq])
    wv = w_in[:, 2 * nq:2 * nq + D_MODEL]
    wo = w_in[:, 2 * nq + D_MODEL:2 * nq + 2 * D_MODEL]
    wg = w_in[:, 2 * nq + 2 * D_MODEL:]
    pad_g = lambda w: jnp.pad(w, ((0, 0), (0, LANES - 2 * N_HEADS)))
    w_cat = jnp.concatenate([wq, wk, wv, wo, pad_g(wg[:, :2 * N_HEADS]), pad_g(wg[:, 2 * N_HEADS:])],
                            axis=1).astype(BF16)
    bias = jnp.pad(b_gates.reshape(2, 1, 2 * N_HEADS).astype(F32), ((0, 0), (0, 0), (0, LANES - 2 * N_HEADS)))
    return w_cat, bias


def kernel(x, positions, l0_norm_mix, l0_w_in, l0_b_gates, l0_head_norm, l0_w_out, l0_norm_ffn, l0_peer_wq, l0_peer_keys, l0_peer_u, l0_peer_v, l1_norm_mix, l1_w_qkv, l1_q_norm, l1_k_norm, l1_lambda_q1, l1_lambda_k1, l1_lambda_q2, l1_lambda_k2, l1_subln, l1_w_out, l1_norm_ffn, l1_peer_wq, l1_peer_keys, l1_peer_u, l1_peer_v):
    b, s, d = x.shape
    t = b * s
    tm = min(512, t)
    x2 = x.reshape(t, d)

    w_cat, bias = _mlstm_layout(l0_w_in, l0_b_gates)
    segs = [(0, D_MODEL), (D_MODEL, D_MODEL), (2 * D_MODEL, D_MODEL), (3 * D_MODEL, D_MODEL),
            (4 * D_MODEL, LANES), (4 * D_MODEL + LANES, LANES)]
    q, k, v, o, g_f, g_b = _norm_proj(x2, l0_norm_mix, w_cat, segs, [BF16, BF16, BF16, F32, F32, F32], tm)
    gates = jnp.stack([g_f, g_b]).reshape(2, b, s, LANES)
    to3 = lambda a: a.reshape(b, s, a.shape[-1])
    h2 = _mlstm(to3(q), to3(k), to3(v), gates, bias, b, s)
    x2 = _mlstm_out(x2, h2.reshape(2, t, D_MODEL), o, l0_head_norm, l0_w_out.astype(BF16), tm)
    x2 = _peer_ffn(x2, l0_norm_ffn, l0_peer_wq, l0_peer_keys, l0_peer_u, l0_peer_v, min(256, t), tm, 512)

    lambda_init = 0.8 - 0.6 * math.exp(-0.3 * 1)
    pos = positions.reshape(t, 1).astype(F32)
    qa, ka, va = _da_proj(x2, l1_norm_mix, l1_w_qkv, l1_q_norm, l1_k_norm, pos, tm)
    lam4 = jnp.stack([l1_lambda_q1, l1_lambda_k1, l1_lambda_q2, l1_lambda_k2]).astype(F32)
    att = _da_attn(to3(qa), to3(ka), to3(va), lam4, l1_subln, b, s, lambda_init, 512)
    x2 = _resid_matmul(x2, att.reshape(t, D_MODEL), l1_w_out.astype(BF16), tm)
    x2 = _peer_ffn(x2, l1_norm_ffn, l1_peer_wq, l1_peer_keys, l1_peer_u, l1_peer_v, min(256, t), tm, 512)
    return x2.reshape(b, s, d)
```

```python
import functools
import math

import jax
import jax.numpy as jnp
from jax import lax
from jax.experimental import pallas as pl
from jax.experimental.pallas import tpu as pltpu

F32 = jnp.float32
BF16 = jnp.bfloat16
HIGHEST = lax.Precision.HIGHEST

D_MODEL = 1024
EPS = 1e-6
LANES = 128
MXU_LANES = 256
N_HEADS = 8
ML_DQK = 64
ML_CHUNK = 128
ML_M_INIT = -1e30
DA_DHEAD = 64
ROPE_THETA = 500000.0
ROPE_HALF = 8
N_KEYS = 128
N_EXPERTS = N_KEYS * N_KEYS
PEER_TOPK = 16
VMEM_LIMIT_BYTES = 56 * 1024 * 1024


def _cparams(sem):
    return pltpu.CompilerParams(dimension_semantics=sem, vmem_limit_bytes=VMEM_LIMIT_BYTES)


def _rms_rows(x, g):
    ms = jnp.mean(x * x, axis=-1, keepdims=True)
    return x * lax.rsqrt(ms + EPS) * g


def _norm_proj_kernel(x_ref, g_ref, w_ref, *out_refs, segs):
    hn = _rms_rows(x_ref[...], g_ref[...]).astype(BF16)
    for o_ref, (start, width) in zip(out_refs, segs):
        o_ref[...] = jnp.dot(hn, w_ref[:, start:start + width],
                             preferred_element_type=F32).astype(o_ref.dtype)


def _norm_proj(x2, g, w_cat, segs, dtypes, tm):
    t = x2.shape[0]
    n = w_cat.shape[1]
    return pl.pallas_call(
        functools.partial(_norm_proj_kernel, segs=tuple(segs)),
        grid=(t // tm,),
        in_specs=[pl.BlockSpec((tm, D_MODEL), lambda i: (i, 0)),
                  pl.BlockSpec((1, D_MODEL), lambda i: (0, 0)),
                  pl.BlockSpec((D_MODEL, n), lambda i: (0, 0))],
        out_specs=[pl.BlockSpec((tm, w), lambda i: (i, 0)) for (_, w) in segs],
        out_shape=[jax.ShapeDtypeStruct((t, w), dt) for (_, w), dt in zip(segs, dtypes)],
        compiler_params=_cparams(("parallel",)),
        name="norm_proj",
    )(x2, g.reshape(1, D_MODEL), w_cat)


def _mlstm_kernel(q_ref, k_ref, v_ref, g_ref, bias_ref, m_ref, mt_ref, o_ref, ct_sc, ms_sc, *, L):
    c = pl.program_id(2)

    @pl.when(c == 0)
    def _():
        ct_sc[...] = jnp.zeros_like(ct_sc)
        ms_sc[...] = jnp.full_like(ms_sc, ML_M_INIT)

    g = g_ref[...] + bias_ref[...]
    lf = jnp.minimum(g, 0.0) - jnp.log1p(jnp.exp(-jnp.abs(g)))
    msk_f = m_ref[...]
    bcol_all = jnp.dot(msk_f, lf, precision=HIGHEST, preferred_element_type=F32)
    g_t = g.T
    brow_all = jnp.dot(lf.T, mt_ref[...], precision=HIGHEST, preferred_element_type=F32)
    allowed = msk_f > 0.5
    lane = lax.broadcasted_iota(jnp.int32, (L, LANES), 1)
    ones_blk = (lane == 0).astype(BF16)

    for h in range(N_HEADS):
        sl = slice(h * LANES, (h + 1) * LANES)
        q = q_ref[:, sl] * (ML_DQK ** -0.5)
        k = k_ref[:, sl]
        v_aug = jnp.concatenate([v_ref[:, sl], ones_blk], axis=1)
        b_col = bcol_all[:, N_HEADS + h:N_HEADS + h + 1]
        b_row = brow_all[N_HEADS + h:N_HEADS + h + 1, :]
        li_col = g[:, h:h + 1]
        li_row = g_t[h:h + 1, :]
        b_tot = jnp.min(b_row, axis=1, keepdims=True)
        m_prev = ms_sc[h]
        ct_prev = ct_sc[h]

        a_col = b_tot - b_col + li_col
        m_loc = jnp.max(a_col, axis=0, keepdims=True)
        w_col = jnp.exp(a_col - m_loc)

        dmat = jnp.where(allowed, b_col - b_row + li_row, -jnp.inf)
        g_col = b_col + m_prev
        m_t = jnp.maximum(g_col, jnp.max(dmat, axis=1, keepdims=True))
        sqk = lax.dot_general(q, k, (((1,), (1,)), ((), ())), preferred_element_type=F32)
        sqk = sqk * jnp.exp(dmat - m_t)
        inter = jnp.exp(g_col - m_t)
        nd = (jnp.dot(sqk.astype(BF16), v_aug, preferred_element_type=F32)
              + inter * jnp.dot(q, ct_prev.astype(BF16), preferred_element_type=F32))
        den = jnp.maximum(jnp.abs(nd[:, LANES:LANES + 1]), jnp.exp(-m_t))
        o_ref[:, sl] = nd[:, :LANES] / den

        m_new = jnp.maximum(b_tot + m_prev, m_loc)
        sp = jnp.exp(b_tot + m_prev - m_new)
        sc = jnp.exp(m_loc - m_new)
        wv = (w_col * v_aug.astype(F32)).astype(BF16)
        ct_loc = lax.dot_general(k, wv, (((0,), (0,)), ((), ())), preferred_element_type=F32)
        ct_sc[h] = sp * ct_prev + sc * ct_loc
        ms_sc[h] = m_new


def _mlstm(q, k, v, gates, bias, b, s):
    L = min(ML_CHUNK, s)
    nc = s // L
    tri = jnp.tril(jnp.ones((L, L), F32))
    masks = jnp.stack([tri, tri.T])
    masks_t = jnp.stack([tri.T, tri])

    def cidx(d, c):
        return c + d * (nc - 1 - 2 * c)

    seq_spec = pl.BlockSpec((None, L, D_MODEL), lambda bi, d, c: (bi, cidx(d, c), 0))
    return pl.pallas_call(
        functools.partial(_mlstm_kernel, L=L),
        grid=(b, 2, nc),
        in_specs=[seq_spec, seq_spec, seq_spec,
                  pl.BlockSpec((None, None, L, LANES), lambda bi, d, c: (d, bi, cidx(d, c), 0)),
                  pl.BlockSpec((None, 1, LANES), lambda bi, d, c: (d, 0, 0)),
                  pl.BlockSpec((None, L, L), lambda bi, d, c: (d, 0, 0)),
                  pl.BlockSpec((None, L, L), lambda bi, d, c: (d, 0, 0))],
        out_specs=pl.BlockSpec((None, None, L, D_MODEL), lambda bi, d, c: (d, bi, cidx(d, c), 0)),
        out_shape=jax.ShapeDtypeStruct((2, b, s, D_MODEL), F32),
        scratch_shapes=[pltpu.VMEM((N_HEADS, LANES, 2 * LANES), F32),
                        pltpu.VMEM((N_HEADS, 1, 1), F32)],
        compiler_params=_cparams(("parallel", "parallel", "arbitrary")),
        name="mlstm_scan",
    )(q, k, v, gates, bias, masks, masks_t)


def _mlstm_out_kernel(x_ref, h_ref, o_ref, hg_ref, w_ref, out_ref):
    hs = h_ref[0] + h_ref[1]
    parts = []
    for h in range(N_HEADS):
        blk = hs[:, h * LANES:(h + 1) * LANES]
        ms = jnp.mean(blk * blk, axis=-1, keepdims=True)
        parts.append(blk * lax.rsqrt(ms + EPS))
    hn = jnp.concatenate(parts, axis=1) * hg_ref[...] * jax.nn.sigmoid(o_ref[...])
    out_ref[...] = x_ref[...] + jnp.dot(hn.astype(BF16), w_ref[...], preferred_element_type=F32)


def _mlstm_out(x2, h2, o, head_g, w_out, tm):
    t = x2.shape[0]
    return pl.pallas_call(
        _mlstm_out_kernel,
        grid=(t // tm,),
        in_specs=[pl.BlockSpec((tm, D_MODEL), lambda i: (i, 0)),
                  pl.BlockSpec((2, tm, D_MODEL), lambda i: (0, i, 0)),
                  pl.BlockSpec((tm, D_MODEL), lambda i: (i, 0)),
                  pl.BlockSpec((1, D_MODEL), lambda i: (0, 0)),
                  pl.BlockSpec((D_MODEL, D_MODEL), lambda i: (0, 0))],
        out_specs=pl.BlockSpec((tm, D_MODEL), lambda i: (i, 0)),
        out_shape=jax.ShapeDtypeStruct((t, D_MODEL), F32),
        compiler_params=_cparams(("parallel",)),
        name="mlstm_out",
    )(x2, h2, o, head_g.reshape(1, D_MODEL), w_out)


def _da_proj_kernel(x_ref, g_ref, w_ref, pos_ref, inv_ref, wts_ref, gq_ref, gk_ref,
                    cq_ref, ck_ref, q_ref, k_ref, v_ref):
    hn = _rms_rows(x_ref[...], g_ref[...]).astype(BF16)
    ang = pos_ref[...] * inv_ref[...]
    cos = jnp.cos(ang)
    sin = jnp.sin(ang)
    trig_q = cq_ref[0:1, :] * cos + cq_ref[1:2, :] * sin + cq_ref[2:3, :]
    trig_k = ck_ref[0:1, :] * cos + ck_ref[1:2, :] * sin + ck_ref[2:3, :]
    fq = trig_q * gq_ref[...]
    fk = trig_k * gk_ref[...]
    nsub = 2 * N_HEADS
    for u in range(nsub):
        for off, fac, wts, o_ref in ((0, fq, wts_ref[0:1, :], q_ref), (nsub * LANES, fk, wts_ref[1:2, :], k_ref)):
            raw = jnp.dot(hn, w_ref[:, off + u * LANES: off + (u + 1) * LANES], preferred_element_type=F32)
            ms = jnp.sum(raw * raw * wts, axis=-1, keepdims=True) * (1.0 / DA_DHEAD)
            o_ref[:, u * LANES:(u + 1) * LANES] = (raw * lax.rsqrt(ms + EPS) * fac).astype(BF16)
    v_ref[...] = jnp.dot(hn, w_ref[:, 2 * nsub * LANES:], preferred_element_type=F32).astype(BF16)


def _da_layout(w_qkv, q_g, k_g):
    x1 = list(range(0, 8))
    x2 = list(range(8, 16))
    rest = list(range(16, 64))
    q_groups = [(x1, 1, 0, 0), (x1, 1, 0, 0), (x2, 0, -1, 0), (x2, 0, -1, 0),
                (x2, 1, 0, 0), (x2, 1, 0, 0), (x1, 0, 1, 0), (x1, 0, 1, 0)]
    k_groups = [(x1, 1, 0, 0), (x2, 0, -1, 0), (x1, 1, 0, 0), (x2, 0, -1, 0),
                (x2, 1, 0, 0), (x1, 0, 1, 0), (x2, 1, 0, 0), (x1, 0, 1, 0)]

    def tables(groups):
        src, coef = [], []
        for dims, a, b_, c_ in groups:
            src += dims
            coef += [(a, b_, c_)] * len(dims)
        src += rest
        coef += [(0, 0, 1)] * len(rest)
        pad = LANES - len(src)
        valid = [1.0] * len(src) + [0.0] * pad
        src += [0] * pad
        coef += [(0, 0, 0)] * pad
        return jnp.array(src, jnp.int32), jnp.array(coef, F32).T, jnp.array(valid, F32)

    src_q, coef_q, valid_q = tables(q_groups)
    src_k, coef_k, valid_k = tables(k_groups)
    nsub = 2 * N_HEADS
    wq = w_qkv[:, :nsub * DA_DHEAD].reshape(D_MODEL, nsub, DA_DHEAD)
    wk = w_qkv[:, nsub * DA_DHEAD:2 * nsub * DA_DHEAD].reshape(D_MODEL, nsub, DA_DHEAD)
    wv = w_qkv[:, 2 * nsub * DA_DHEAD:]
    wq_p = (wq[:, :, src_q] * valid_q).reshape(D_MODEL, nsub * LANES)
    wk_p = (wk[:, :, src_k] * valid_k).reshape(D_MODEL, nsub * LANES)
    w_cat = jnp.concatenate([wq_p, wk_p, wv], axis=1).astype(BF16)
    lanes = jnp.arange(LANES)
    plain = (lanes >= 64) & (lanes < 112)
    wts_q = ((lanes < 8) | ((lanes >= 16) & (lanes < 24)) | plain).astype(F32)
    wts_k = ((lanes < 16) | plain).astype(F32)
    wts = jnp.concatenate([wts_q[None], wts_k[None], jnp.zeros((6, LANES), F32)], axis=0)
    gq = (q_g[src_q] * valid_q).reshape(1, LANES) * (DA_DHEAD ** -0.5)
    gk = (k_g[src_k] * valid_k).reshape(1, LANES)
    freq = jnp.where(lanes < 64, lanes % ROPE_HALF, 0)
    inv = ROPE_THETA ** (-freq.astype(F32) * 2.0 / (2 * ROPE_HALF))
    inv = jnp.where(lanes < 64, inv, 0.0).reshape(1, LANES)
    pad8 = lambda a: jnp.concatenate([a, jnp.zeros((5, LANES), F32)], axis=0)
    return w_cat, inv, wts, gq, gk, pad8(coef_q), pad8(coef_k)


def _da_proj(x2, norm_g, w_qkv, q_g, k_g, pos, tm):
    t = x2.shape[0]
    w_cat, inv, wts, gq, gk, cq, ck = _da_layout(w_qkv, q_g, k_g)
    nq = 2 * N_HEADS * LANES
    const = lambda shp: pl.BlockSpec(shp, lambda i: (0,) * len(shp))
    return pl.pallas_call(
        _da_proj_kernel,
        grid=(t // tm,),
        in_specs=[pl.BlockSpec((tm, D_MODEL), lambda i: (i, 0)),
                  const((1, D_MODEL)), const(w_cat.shape),
                  pl.BlockSpec((tm, 1), lambda i: (i, 0)),
                  const((1, LANES)), const((8, LANES)), const((1, LANES)), const((1, LANES)),
                  const((8, LANES)), const((8, LANES))],
        out_specs=[pl.BlockSpec((tm, nq), lambda i: (i, 0)),
                   pl.BlockSpec((tm, nq), lambda i: (i, 0)),
                   pl.BlockSpec((tm, D_MODEL), lambda i: (i, 0))],
        out_shape=[jax.ShapeDtypeStruct((t, nq), BF16),
                   jax.ShapeDtypeStruct((t, nq), BF16),
                   jax.ShapeDtypeStruct((t, D_MODEL), BF16)],
        compiler_params=_cparams(("parallel",)),
        name="da_proj",
    )(x2, norm_g.reshape(1, D_MODEL), w_cat, pos, inv, wts, gq, gk, cq, ck)


def _da_attn_kernel(q_ref, k_ref, v_ref, lam_ref, sg_ref, o_ref, *, lambda_init):
    out_scale = 1.0 - lambda_init
    lam_v = lam_ref[...]
    lam = (jnp.exp(jnp.sum(lam_v[0:1] * lam_v[1:2], axis=-1, keepdims=True))
           - jnp.exp(jnp.sum(lam_v[2:3] * lam_v[3:4], axis=-1, keepdims=True))
           + lambda_init)
    v = v_ref[...]
    outs = []
    for c in range(2):
        q = q_ref[:, c * LANES:(c + 1) * LANES]
        k = k_ref[:, c * LANES:(c + 1) * LANES]
        s = lax.dot_general(q, k, (((1,), (1,)), ((), ())), preferred_element_type=F32)
        m = jnp.max(s, axis=-1, keepdims=True)
        p = jnp.exp(s - m)
        l = jnp.sum(p, axis=-1, keepdims=True)
        outs.append(jnp.dot(p.astype(BF16), v, preferred_element_type=F32) / l)
    o = outs[0] - lam * outs[1]
    ms = jnp.mean(o * o, axis=-1, keepdims=True)
    o_ref[...] = (o * lax.rsqrt(ms + EPS) * sg_ref[...]) * out_scale


def _da_attn(q, k, v, lam4, subln_g, b, s, lambda_init, tq):
    tq = min(tq, s)
    return pl.pallas_call(
        functools.partial(_da_attn_kernel, lambda_init=lambda_init),
        grid=(b, N_HEADS, s // tq),
        in_specs=[pl.BlockSpec((None, tq, 2 * LANES), lambda bi, h, i: (bi, i, h)),
                  pl.BlockSpec((None, s, 2 * LANES), lambda bi, h, i: (bi, 0, h)),
                  pl.BlockSpec((None, s, LANES), lambda bi, h, i: (bi, 0, h)),
                  pl.BlockSpec((4, DA_DHEAD), lambda bi, h, i: (0, 0)),
                  pl.BlockSpec((1, LANES), lambda bi, h, i: (0, 0))],
        out_specs=pl.BlockSpec((None, tq, LANES), lambda bi, h, i: (bi, i, h)),
        out_shape=jax.ShapeDtypeStruct((b, s, D_MODEL), F32),
        compiler_params=_cparams(("parallel", "parallel", "arbitrary")),
        name="da_attn",
    )(q, k, v, lam4, subln_g.reshape(1, LANES))


def _resid_matmul_kernel(x_ref, a_ref, w_ref, out_ref):
    out_ref[...] = x_ref[...] + jnp.dot(a_ref[...].astype(BF16), w_ref[...], preferred_element_type=F32)


def _resid_matmul(x2, a2, w, tm):
    t = x2.shape[0]
    return pl.pallas_call(
        _resid_matmul_kernel,
        grid=(t // tm,),
        in_specs=[pl.BlockSpec((tm, D_MODEL), lambda i: (i, 0)),
                  pl.BlockSpec((tm, D_MODEL), lambda i: (i, 0)),
                  pl.BlockSpec((D_MODEL, D_MODEL), lambda i: (0, 0))],
        out_specs=pl.BlockSpec((tm, D_MODEL), lambda i: (i, 0)),
        out_shape=jax.ShapeDtypeStruct((t, D_MODEL), F32),
        compiler_params=_cparams(("parallel",)),
        name="resid_matmul",
    )(x2, a2, w)


N_EXTRACT = PEER_TOPK + 1
A_ROWS = 24
N_FULL_ROWS = 8


RANK_NONE = 64.0


def _extract_desc(vals, a_sc):
    rank = jnp.full(vals.shape, RANK_NONE, F32)
    for kk in range(N_EXTRACT):
        mx = jnp.max(vals, axis=0, keepdims=True)
        a_sc[kk:kk + 1, :] = mx
        hit = vals == mx
        rank = jnp.where(hit, float(kk), rank)
        vals = jnp.where(hit, -jnp.inf, vals)
    return rank


def _dup_bf16_words(x):
    hi = lax.bitcast_convert_type(x.astype(BF16).astype(F32), jnp.uint32)
    return hi | (hi >> 16)


def _peer_route_kernel(x_ref, g_ref, wq_ref, keys_ref, hnt_ref, cnt_ref, e1z_ref, r2_ref, e2_ref,
                       a1_sc, a2_sc):
    tm = x_ref.shape[0]
    hn = _rms_rows(x_ref[...], g_ref[...])
    hnt_ref[...] = hn.T.astype(BF16)
    q = jnp.dot(hn.astype(BF16), wq_ref[...], preferred_element_type=F32).astype(BF16)
    neg = jnp.full((A_ROWS, tm), -jnp.inf, F32)
    row = lax.broadcasted_iota(jnp.int32, (A_ROWS, tm), 0)
    for h in range(N_HEADS):
        s_t = []
        for p, a_sc in ((0, a1_sc), (1, a2_sc)):
            hp = 2 * h + p
            st = lax.dot_general(keys_ref[hp], q[:, hp * LANES:(hp + 1) * LANES],
                                 (((1,), (1,)), ((), ())), preferred_element_type=F32)
            a_sc[...] = neg
            rank2 = _extract_desc(st, a_sc)
            s_t.append(st)
        a1 = a1_sc[...]
        a2 = a2_sc[...]
        blocks = [a1[p:p + 1, :] + a2 for p in range(N_FULL_ROWS)]
        blocks.append(jnp.where(row >= N_FULL_ROWS, a1 + a2[0:1, :], -jnp.inf))
        cvals = []
        for kk in range(N_EXTRACT):
            cur = blocks[0]
            for blk in blocks[1:]:
                cur = jnp.maximum(cur, blk)
            mx = jnp.max(cur, axis=0, keepdims=True)
            cvals.append(mx)
            blocks = [jnp.where(blk == mx, -jnp.inf, blk) for blk in blocks]
        c0 = cvals[0]
        z = jnp.ones_like(c0)
        for kk in range(1, PEER_TOPK):
            z = z + jnp.exp(cvals[kk] - c0)
        tau = 0.5 * (cvals[PEER_TOPK - 1] + cvals[PEER_TOPK])
        s1, s2 = s_t
        thr = tau - s1
        cnt = jnp.zeros_like(s1)
        for qq in range(N_EXTRACT):
            cnt = cnt + jnp.where(a2[qq:qq + 1, :] >= thr, 1.0, 0.0)
        cnt_ref[h] = _dup_bf16_words(cnt)
        e1z_ref[h] = _dup_bf16_words(jnp.exp(s1 - a1[0:1, :]) * (0.5 / z))
        r2_ref[h] = pltpu.bitcast(rank2.astype(BF16), jnp.uint32)
        e2_ref[h] = pltpu.bitcast(jnp.exp(s2 - a2[0:1, :]).astype(BF16), jnp.uint32)


def _peer_route(x2, norm_g, wq, keys, tm):
    t = x2.shape[0]
    row_spec = pl.BlockSpec((N_HEADS, N_KEYS, tm), lambda i: (0, 0, i))
    pair_spec = pl.BlockSpec((N_HEADS, N_KEYS // 2, tm), lambda i: (0, 0, i))
    row_tab = jax.ShapeDtypeStruct((N_HEADS, N_KEYS, t), jnp.uint32)
    pair_tab = jax.ShapeDtypeStruct((N_HEADS, N_KEYS // 2, t), jnp.uint32)
    return pl.pallas_call(
        _peer_route_kernel,
        grid=(t // tm,),
        in_specs=[pl.BlockSpec((tm, D_MODEL), lambda i: (i, 0)),
                  pl.BlockSpec((1, D_MODEL), lambda i: (0, 0)),
                  pl.BlockSpec(wq.shape, lambda i: (0, 0)),
                  pl.BlockSpec(keys.shape, lambda i: (0, 0, 0))],
        out_specs=[pl.BlockSpec((D_MODEL, tm), lambda i: (0, i)), row_spec, row_spec, pair_spec, pair_spec],
        out_shape=[jax.ShapeDtypeStruct((D_MODEL, t), BF16), row_tab, row_tab, pair_tab, pair_tab],
        scratch_shapes=[pltpu.VMEM((A_ROWS, tm), F32), pltpu.VMEM((A_ROWS, tm), F32)],
        compiler_params=_cparams(("parallel",)),
        name="peer_route",
    )(x2, norm_g.reshape(1, D_MODEL), wq, keys)


PIPE_STAGES = 3
I_GROUP = 2


def _peer_expert_kernel(hnt_ref, u_ref, vt_ref, cnt_ref, e1z_ref, r2_ref, e2_ref, x_ref, out_ref,
                        acc_sc, s0_sc, s1_sc, p0_sc, p1_sc, *, ec, n_chunks):
    g = pl.program_id(1)
    tm = hnt_ref.shape[1]
    n_i = ec // N_KEYS

    @pl.when(g == 0)
    def _():
        acc_sc[...] = jnp.zeros_like(acc_sc)
        s1_sc[...] = jnp.zeros_like(s1_sc)
        p1_sc[...] = jnp.zeros_like(p1_sc)

    def row_bf16(ref, h, i, ls):
        words = jnp.broadcast_to(ref[h, pl.ds(i, 1), :][:, ls], (N_KEYS // 2, LANES))
        return pltpu.bitcast(words, BF16)

    zero = jnp.zeros((), BF16)

    def gate_tile(ci, lt, s_in, p_out):
        ls = slice(lt * LANES, (lt + 1) * LANES)
        for i0 in range(0, n_i, I_GROUP):
            gates = [jnp.zeros((N_KEYS, LANES), BF16) for _ in range(I_GROUP)]
            for h in range(N_HEADS):
                r2t = pltpu.bitcast(r2_ref[h][:, ls], BF16)
                e2t = pltpu.bitcast(e2_ref[h][:, ls], BF16)
                for kk in range(I_GROUP):
                    i = ci * n_i + i0 + kk
                    sel = jnp.where(r2t < row_bf16(cnt_ref, h, i, ls), e2t, zero)
                    gates[kk] = gates[kk] + sel * row_bf16(e1z_ref, h, i, ls)
            for kk in range(I_GROUP):
                r0 = (i0 + kk) * N_KEYS
                s = s_in[r0:r0 + N_KEYS, ls].astype(BF16)
                act2 = s * (1.0 + lax.erf(s * (2.0 ** -0.5)))
                p_out[r0 // 2:(r0 + N_KEYS) // 2, ls] = pltpu.bitcast(gates[kk] * act2, jnp.uint32)

    def pipeline_step(c, u_blk, vt_blk, s_in, s_out, p_in, p_out):
        ci = jnp.clip(c - 1, 0, n_chunks - 1)
        tiles_per_slab = MXU_LANES // LANES
        for sb in range(tm // MXU_LANES):
            cs = slice(sb * MXU_LANES, (sb + 1) * MXU_LANES)
            acc_sc[:, cs] += jnp.dot(vt_blk, pltpu.bitcast(p_in[:, cs], BF16), preferred_element_type=F32)
            gate_tile(ci, sb * tiles_per_slab, s_in, p_out)
            s_out[:, cs] = jnp.dot(u_blk, hnt_ref[:, cs], preferred_element_type=F32)
            for lt in range(sb * tiles_per_slab + 1, (sb + 1) * tiles_per_slab):
                gate_tile(ci, lt, s_in, p_out)

    pipeline_step(2 * g, u_ref[:ec, :], vt_ref[:, :ec], s1_sc, s0_sc, p1_sc, p0_sc)
    pipeline_step(2 * g + 1, u_ref[ec:, :], vt_ref[:, ec:], s0_sc, s1_sc, p0_sc, p1_sc)

    @pl.when(g == pl.num_programs(1) - 1)
    def _():
        out_ref[...] = x_ref[...] + acc_sc[...].T


def _peer_expert(x2, hnt, u, vt, cnt, e1z, r2, e2, tm, ec):
    t = x2.shape[0]
    n_chunks = N_EXPERTS // ec
    n_steps = (n_chunks + PIPE_STAGES - 1) // 2
    last_u = n_chunks // 2 - 1
    row_spec = pl.BlockSpec((N_HEADS, N_KEYS, tm), lambda i, g: (0, 0, i))
    pair_spec = pl.BlockSpec((N_HEADS, N_KEYS // 2, tm), lambda i, g: (0, 0, i))
    return pl.pallas_call(
        functools.partial(_peer_expert_kernel, ec=ec, n_chunks=n_chunks),
        grid=(t // tm, n_steps),
        in_specs=[pl.BlockSpec((D_MODEL, tm), lambda i, g: (0, i)),
                  pl.BlockSpec((2 * ec, D_MODEL), lambda i, g: (jnp.minimum(g, last_u), 0)),
                  pl.BlockSpec((D_MODEL, 2 * ec), lambda i, g: (0, jnp.maximum(g - 1, 0))),
                  row_spec, row_spec, pair_spec, pair_spec,
                  pl.BlockSpec((tm, D_MODEL), lambda i, g: (i, 0))],
        out_specs=pl.BlockSpec((tm, D_MODEL), lambda i, g: (i, 0)),
        out_shape=jax.ShapeDtypeStruct((t, D_MODEL), F32),
        scratch_shapes=[pltpu.VMEM((D_MODEL, tm), F32),
                        pltpu.VMEM((ec, tm), F32), pltpu.VMEM((ec, tm), F32),
                        pltpu.VMEM((ec // 2, tm), jnp.uint32), pltpu.VMEM((ec // 2, tm), jnp.uint32)],
        compiler_params=_cparams(("parallel", "arbitrary")),
        name="peer_expert",
    )(hnt, u, vt, cnt, e1z, r2, e2, x2)


def _peer_ffn(x2, norm_g, w_query, sub_keys, u_table, v_table, tm_route, tm_exp, ec):
    keys = sub_keys.reshape(2 * N_HEADS, N_KEYS, LANES).astype(BF16)
    hnt, cnt, e1z, r2, e2 = _peer_route(x2, norm_g, w_query.astype(BF16), keys, tm_route)
    return _peer_expert(x2, hnt, u_table.astype(BF16), v_table.T.astype(BF16), cnt, e1z, r2, e2, tm_exp, ec)


def _mlstm_layout(w_in, b_gates):
    nq = N_HEADS * ML_DQK
    pad_heads = lambda w: jnp.pad(w.reshape(D_MODEL, N_HEADS, ML_DQK),
                                  ((0, 0), (0, 0), (0, LANES - ML_DQK))).reshape(D_MODEL, N_HEADS * LANES)
    wq = pad_heads(w_in[:, :nq])
    wk = pad_heads(w_in[:, nq:2 * nq])
    wv = w_in[:, 2 * nq:2 * nq + D_MODEL]
    wo = w_in[:, 2 * nq + D_MODEL:2 * nq + 2 * D_MODEL]
    wg = w_in[:, 2 * nq + 2 * D_MODEL:]
    pad_g = lambda w: jnp.pad(w, ((0, 0), (0, LANES - 2 * N_HEADS)))
    w_cat = jnp.concatenate([wq, wk, wv, wo, pad_g(wg[:, :2 * N_HEADS]), pad_g(wg[:, 2 * N_HEADS:])],
                            axis=1).astype(BF16)
    bias = jnp.pad(b_gates.reshape(2, 1, 2 * N_HEADS).astype(F32), ((0, 0), (0, 0), (0, LANES - 2 * N_HEADS)))
    return w_cat, bias


def kernel(x, positions, l0_norm_mix, l0_w_in, l0_b_gates, l0_head_norm, l0_w_out, l0_norm_ffn, l0_peer_wq, l0_peer_keys, l0_peer_u, l0_peer_v, l1_norm_mix, l1_w_qkv, l1_q_norm, l1_k_norm, l1_lambda_q1, l1_lambda_k1, l1_lambda_q2, l1_lambda_k2, l1_subln, l1_w_out, l1_norm_ffn, l1_peer_wq, l1_peer_keys, l1_peer_u, l1_peer_v):
    b, s, d = x.shape
    t = b * s
    tm = min(512, t)
    x2 = x.reshape(t, d)

    w_cat, bias = _mlstm_layout(l0_w_in, l0_b_gates)
    segs = [(0, D_MODEL), (D_MODEL, D_MODEL), (2 * D_MODEL, D_MODEL), (3 * D_MODEL, D_MODEL),
            (4 * D_MODEL, LANES), (4 * D_MODEL + LANES, LANES)]
    q, k, v, o, g_f, g_b = _norm_proj(x2, l0_norm_mix, w_cat, segs, [BF16, BF16, BF16, F32, F32, F32], tm)
    gates = jnp.stack([g_f, g_b]).reshape(2, b, s, LANES)
    to3 = lambda a: a.reshape(b, s, a.shape[-1])
    h2 = _mlstm(to3(q), to3(k), to3(v), gates, bias, b, s)
    x2 = _mlstm_out(x2, h2.reshape(2, t, D_MODEL), o, l0_head_norm, l0_w_out.astype(BF16), tm)
    x2 = _peer_ffn(x2, l0_norm_ffn, l0_peer_wq, l0_peer_keys, l0_peer_u, l0_peer_v, min(256, t), tm, 512)

    lambda_init = 0.8 - 0.6 * math.exp(-0.3 * 1)
    pos = positions.reshape(t, 1).astype(F32)
    qa, ka, va = _da_proj(x2, l1_norm_mix, l1_w_qkv, l1_q_norm, l1_k_norm, pos, tm)
    lam4 = jnp.stack([l1_lambda_q1, l1_lambda_k1, l1_lambda_q2, l1_lambda_k2]).astype(F32)
    att = _da_attn(to3(qa), to3(ka), to3(va), lam4, l1_subln, b, s, lambda_init, 512)
    x2 = _resid_matmul(x2, att.reshape(t, D_MODEL), l1_w_out.astype(BF16), tm)
    x2 = _peer_ffn(x2, l1_norm_ffn, l1_peer_wq, l1_peer_keys, l1_peer_u, l1_peer_v, min(256, t), tm, 512)
    return x2.reshape(b, s, d)
```

```python
import functools
import math

import jax
import jax.numpy as jnp
from jax import lax
from jax.experimental import pallas as pl
from jax.experimental.pallas import tpu as pltpu

F32 = jnp.float32
BF16 = jnp.bfloat16
HIGHEST = lax.Precision.HIGHEST

D_MODEL = 1024
EPS = 1e-6
LANES = 128
SUBLANES = 8
MXU_LANES = 256
N_HEADS = 8
ML_DQK = 64
ML_CHUNK = 128
ML_M_INIT = -1e30
DA_DHEAD = 64
ROPE_THETA = 500000.0
ROPE_HALF = 8
N_KEYS = 128
N_EXPERTS = N_KEYS * N_KEYS
PEER_TOPK = 16
VMEM_LIMIT_BYTES = 56 * 1024 * 1024


def _cparams(sem):
    return pltpu.CompilerParams(dimension_semantics=sem, vmem_limit_bytes=VMEM_LIMIT_BYTES)


def _rms_rows(x, g):
    ms = jnp.mean(x * x, axis=-1, keepdims=True)
    return x * lax.rsqrt(ms + EPS) * g


def _norm_proj_kernel(x_ref, g_ref, w_ref, *out_refs, segs):
    hn = _rms_rows(x_ref[...], g_ref[...]).astype(BF16)
    for o_ref, (start, width) in zip(out_refs, segs):
        o_ref[...] = jnp.dot(hn, w_ref[:, start:start + width],
                             preferred_element_type=F32).astype(o_ref.dtype)


def _norm_proj(x2, g, w_cat, segs, dtypes, tm):
    t = x2.shape[0]
    n = w_cat.shape[1]
    return pl.pallas_call(
        functools.partial(_norm_proj_kernel, segs=tuple(segs)),
        grid=(t // tm,),
        in_specs=[pl.BlockSpec((tm, D_MODEL), lambda i: (i, 0)),
                  pl.BlockSpec((1, D_MODEL), lambda i: (0, 0)),
                  pl.BlockSpec((D_MODEL, n), lambda i: (0, 0))],
        out_specs=[pl.BlockSpec((tm, w), lambda i: (i, 0)) for (_, w) in segs],
        out_shape=[jax.ShapeDtypeStruct((t, w), dt) for (_, w), dt in zip(segs, dtypes)],
        compiler_params=_cparams(("parallel",)),
        name="norm_proj",
    )(x2, g.reshape(1, D_MODEL), w_cat)


def _mlstm_kernel(q_ref, k_ref, v_ref, g_ref, bias_ref, m_ref, mt_ref, o_ref, ct_sc, ms_sc, *, L):
    c = pl.program_id(2)

    @pl.when(c == 0)
    def _():
        ct_sc[...] = jnp.zeros_like(ct_sc)
        ms_sc[...] = jnp.full_like(ms_sc, ML_M_INIT)

    g = g_ref[...] + bias_ref[...]
    lf = jnp.minimum(g, 0.0) - jnp.log1p(jnp.exp(-jnp.abs(g)))
    msk_f = m_ref[...]
    bcol_all = jnp.dot(msk_f, lf, precision=HIGHEST, preferred_element_type=F32)
    g_t = g.T
    brow_all = jnp.dot(lf.T, mt_ref[...], precision=HIGHEST, preferred_element_type=F32)
    allowed = msk_f > 0.5
    lane = lax.broadcasted_iota(jnp.int32, (L, LANES), 1)
    ones_blk = (lane == 0).astype(BF16)

    for h in range(N_HEADS):
        sl = slice(h * LANES, (h + 1) * LANES)
        q = q_ref[:, sl] * (ML_DQK ** -0.5)
        k = k_ref[:, sl]
        v_aug = jnp.concatenate([v_ref[:, sl], ones_blk], axis=1)
        b_col = bcol_all[:, N_HEADS + h:N_HEADS + h + 1]
        b_row = brow_all[N_HEADS + h:N_HEADS + h + 1, :]
        li_col = g[:, h:h + 1]
        li_row = g_t[h:h + 1, :]
        b_tot = jnp.min(b_row, axis=1, keepdims=True)
        m_prev = ms_sc[h]
        ct_prev = ct_sc[h]

        a_col = b_tot - b_col + li_col
        m_loc = jnp.max(a_col, axis=0, keepdims=True)
        w_col = jnp.exp(a_col - m_loc)

        dmat = jnp.where(allowed, b_col - b_row + li_row, -jnp.inf)
        g_col = b_col + m_prev
        m_t = jnp.maximum(g_col, jnp.max(dmat, axis=1, keepdims=True))
        sqk = lax.dot_general(q, k, (((1,), (1,)), ((), ())), preferred_element_type=F32)
        sqk = sqk * jnp.exp(dmat - m_t)
        inter = jnp.exp(g_col - m_t)
        nd = (jnp.dot(sqk.astype(BF16), v_aug, preferred_element_type=F32)
              + inter * jnp.dot(q, ct_prev.astype(BF16), preferred_element_type=F32))
        den = jnp.maximum(jnp.abs(nd[:, LANES:LANES + 1]), jnp.exp(-m_t))
        o_ref[:, sl] = nd[:, :LANES] / den

        m_new = jnp.maximum(b_tot + m_prev, m_loc)
        sp = jnp.exp(b_tot + m_prev - m_new)
        sc = jnp.exp(m_loc - m_new)
        wv = (w_col * v_aug.astype(F32)).astype(BF16)
        ct_loc = lax.dot_general(k, wv, (((0,), (0,)), ((), ())), preferred_element_type=F32)
        ct_sc[h] = sp * ct_prev + sc * ct_loc
        ms_sc[h] = m_new


def _mlstm(q, k, v, gates, bias, b, s):
    L = min(ML_CHUNK, s)
    nc = s // L
    tri = jnp.tril(jnp.ones((L, L), F32))
    masks = jnp.stack([tri, tri.T])
    masks_t = jnp.stack([tri.T, tri])

    def cidx(d, c):
        return c + d * (nc - 1 - 2 * c)

    seq_spec = pl.BlockSpec((None, L, D_MODEL), lambda bi, d, c: (bi, cidx(d, c), 0))
    return pl.pallas_call(
        functools.partial(_mlstm_kernel, L=L),
        grid=(b, 2, nc),
        in_specs=[seq_spec, seq_spec, seq_spec,
                  pl.BlockSpec((None, None, L, LANES), lambda bi, d, c: (d, bi, cidx(d, c), 0)),
                  pl.BlockSpec((None, 1, LANES), lambda bi, d, c: (d, 0, 0)),
                  pl.BlockSpec((None, L, L), lambda bi, d, c: (d, 0, 0)),
                  pl.BlockSpec((None, L, L), lambda bi, d, c: (d, 0, 0))],
        out_specs=pl.BlockSpec((None, None, L, D_MODEL), lambda bi, d, c: (d, bi, cidx(d, c), 0)),
        out_shape=jax.ShapeDtypeStruct((2, b, s, D_MODEL), F32),
        scratch_shapes=[pltpu.VMEM((N_HEADS, LANES, 2 * LANES), F32),
                        pltpu.VMEM((N_HEADS, 1, 1), F32)],
        compiler_params=_cparams(("parallel", "parallel", "arbitrary")),
        name="mlstm_scan",
    )(q, k, v, gates, bias, masks, masks_t)


def _mlstm_out_kernel(x_ref, h_ref, o_ref, hg_ref, w_ref, out_ref):
    hs = h_ref[0] + h_ref[1]
    parts = []
    for h in range(N_HEADS):
        blk = hs[:, h * LANES:(h + 1) * LANES]
        ms = jnp.mean(blk * blk, axis=-1, keepdims=True)
        parts.append(blk * lax.rsqrt(ms + EPS))
    hn = jnp.concatenate(parts, axis=1) * hg_ref[...] * jax.nn.sigmoid(o_ref[...])
    out_ref[...] = x_ref[...] + jnp.dot(hn.astype(BF16), w_ref[...], preferred_element_type=F32)


def _mlstm_out(x2, h2, o, head_g, w_out, tm):
    t = x2.shape[0]
    return pl.pallas_call(
        _mlstm_out_kernel,
        grid=(t // tm,),
        in_specs=[pl.BlockSpec((tm, D_MODEL), lambda i: (i, 0)),
                  pl.BlockSpec((2, tm, D_MODEL), lambda i: (0, i, 0)),
                  pl.BlockSpec((tm, D_MODEL), lambda i: (i, 0)),
                  pl.BlockSpec((1, D_MODEL), lambda i: (0, 0)),
                  pl.BlockSpec((D_MODEL, D_MODEL), lambda i: (0, 0))],
        out_specs=pl.BlockSpec((tm, D_MODEL), lambda i: (i, 0)),
        out_shape=jax.ShapeDtypeStruct((t, D_MODEL), F32),
        compiler_params=_cparams(("parallel",)),
        name="mlstm_out",
    )(x2, h2, o, head_g.reshape(1, D_MODEL), w_out)


def _da_proj_kernel(x_ref, g_ref, w_ref, pos_ref, inv_ref, wts_ref, gq_ref, gk_ref,
                    cq_ref, ck_ref, q_ref, k_ref, v_ref):
    hn = _rms_rows(x_ref[...], g_ref[...]).astype(BF16)
    ang = pos_ref[...] * inv_ref[...]
    cos = jnp.cos(ang)
    sin = jnp.sin(ang)
    trig_q = cq_ref[0:1, :] * cos + cq_ref[1:2, :] * sin + cq_ref[2:3, :]
    trig_k = ck_ref[0:1, :] * cos + ck_ref[1:2, :] * sin + ck_ref[2:3, :]
    fq = trig_q * gq_ref[...]
    fk = trig_k * gk_ref[...]
    nsub = 2 * N_HEADS
    for u in range(nsub):
        for off, fac, wts, o_ref in ((0, fq, wts_ref[0:1, :], q_ref), (nsub * LANES, fk, wts_ref[1:2, :], k_ref)):
            raw = jnp.dot(hn, w_ref[:, off + u * LANES: off + (u + 1) * LANES], preferred_element_type=F32)
            ms = jnp.sum(raw * raw * wts, axis=-1, keepdims=True) * (1.0 / DA_DHEAD)
            o_ref[:, u * LANES:(u + 1) * LANES] = (raw * lax.rsqrt(ms + EPS) * fac).astype(BF16)
    v_ref[...] = jnp.dot(hn, w_ref[:, 2 * nsub * LANES:], preferred_element_type=F32).astype(BF16)


def _da_layout(w_qkv, q_g, k_g):
    x1 = list(range(0, 8))
    x2 = list(range(8, 16))
    rest = list(range(16, 64))
    q_groups = [(x1, 1, 0, 0), (x1, 1, 0, 0), (x2, 0, -1, 0), (x2, 0, -1, 0),
                (x2, 1, 0, 0), (x2, 1, 0, 0), (x1, 0, 1, 0), (x1, 0, 1, 0)]
    k_groups = [(x1, 1, 0, 0), (x2, 0, -1, 0), (x1, 1, 0, 0), (x2, 0, -1, 0),
                (x2, 1, 0, 0), (x1, 0, 1, 0), (x2, 1, 0, 0), (x1, 0, 1, 0)]

    def tables(groups):
        src, coef = [], []
        for dims, a, b_, c_ in groups:
            src += dims
            coef += [(a, b_, c_)] * len(dims)
        src += rest
        coef += [(0, 0, 1)] * len(rest)
        pad = LANES - len(src)
        valid = [1.0] * len(src) + [0.0] * pad
        src += [0] * pad
        coef += [(0, 0, 0)] * pad
        return jnp.array(src, jnp.int32), jnp.array(coef, F32).T, jnp.array(valid, F32)

    src_q, coef_q, valid_q = tables(q_groups)
    src_k, coef_k, valid_k = tables(k_groups)
    nsub = 2 * N_HEADS
    wq = w_qkv[:, :nsub * DA_DHEAD].reshape(D_MODEL, nsub, DA_DHEAD)
    wk = w_qkv[:, nsub * DA_DHEAD:2 * nsub * DA_DHEAD].reshape(D_MODEL, nsub, DA_DHEAD)
    wv = w_qkv[:, 2 * nsub * DA_DHEAD:]
    wq_p = (wq[:, :, src_q] * valid_q).reshape(D_MODEL, nsub * LANES)
    wk_p = (wk[:, :, src_k] * valid_k).reshape(D_MODEL, nsub * LANES)
    w_cat = jnp.concatenate([wq_p, wk_p, wv], axis=1).astype(BF16)
    lanes = jnp.arange(LANES)
    plain = (lanes >= 64) & (lanes < 112)
    wts_q = ((lanes < 8) | ((lanes >= 16) & (lanes < 24)) | plain).astype(F32)
    wts_k = ((lanes < 16) | plain).astype(F32)
    wts = jnp.concatenate([wts_q[None], wts_k[None], jnp.zeros((6, LANES), F32)], axis=0)
    gq = (q_g[src_q] * valid_q).reshape(1, LANES) * (DA_DHEAD ** -0.5 * math.log2(math.e))
    gk = (k_g[src_k] * valid_k).reshape(1, LANES)
    freq = jnp.where(lanes < 64, lanes % ROPE_HALF, 0)
    inv = ROPE_THETA ** (-freq.astype(F32) * 2.0 / (2 * ROPE_HALF))
    inv = jnp.where(lanes < 64, inv, 0.0).reshape(1, LANES)
    pad8 = lambda a: jnp.concatenate([a, jnp.zeros((5, LANES), F32)], axis=0)
    return w_cat, inv, wts, gq, gk, pad8(coef_q), pad8(coef_k)


def _da_proj(x2, norm_g, w_qkv, q_g, k_g, pos, tm):
    t = x2.shape[0]
    w_cat, inv, wts, gq, gk, cq, ck = _da_layout(w_qkv, q_g, k_g)
    nq = 2 * N_HEADS * LANES
    const = lambda shp: pl.BlockSpec(shp, lambda i: (0,) * len(shp))
    return pl.pallas_call(
        _da_proj_kernel,
        grid=(t // tm,),
        in_specs=[pl.BlockSpec((tm, D_MODEL), lambda i: (i, 0)),
                  const((1, D_MODEL)), const(w_cat.shape),
                  pl.BlockSpec((tm, 1), lambda i: (i, 0)),
                  const((1, LANES)), const((8, LANES)), const((1, LANES)), const((1, LANES)),
                  const((8, LANES)), const((8, LANES))],
        out_specs=[pl.BlockSpec((tm, nq), lambda i: (i, 0)),
                   pl.BlockSpec((tm, nq), lambda i: (i, 0)),
                   pl.BlockSpec((tm, D_MODEL), lambda i: (i, 0))],
        out_shape=[jax.ShapeDtypeStruct((t, nq), BF16),
                   jax.ShapeDtypeStruct((t, nq), BF16),
                   jax.ShapeDtypeStruct((t, D_MODEL), BF16)],
        compiler_params=_cparams(("parallel",)),
        name="da_proj",
    )(x2, norm_g.reshape(1, D_MODEL), w_cat, pos, inv, wts, gq, gk, cq, ck)


def _da_attn_kernel(q_ref, k_ref, v_ref, lam_ref, sg_ref, o_ref, s_sc, *, lambda_init, kb):
    out_scale = 1.0 - lambda_init
    lam_v = lam_ref[...]
    lam = (jnp.exp(jnp.sum(lam_v[0:1] * lam_v[1:2], axis=-1, keepdims=True))
           - jnp.exp(jnp.sum(lam_v[2:3] * lam_v[3:4], axis=-1, keepdims=True))
           + lambda_init)
    tq = q_ref.shape[0]
    n_blk = k_ref.shape[0] // kb
    subs = (0, 1)
    q = [q_ref[:, c * LANES:(c + 1) * LANES] for c in subs]
    m_part = [jnp.full((tq, LANES), -jnp.inf, F32) for _ in subs]
    for blk in range(n_blk):
        rows = slice(blk * kb, (blk + 1) * kb)
        for c in subs:
            s = lax.dot_general(q[c], k_ref[rows, c * LANES:(c + 1) * LANES], (((1,), (1,)), ((), ())),
                                preferred_element_type=F32)
            s_sc[c, :, rows] = s
            for j in range(kb // LANES):
                m_part[c] = jnp.maximum(m_part[c], s[:, j * LANES:(j + 1) * LANES])
    m = [jnp.max(m_part[c], axis=-1, keepdims=True) for c in subs]
    l_part = [jnp.zeros((tq, LANES), F32) for _ in subs]
    acc = [jnp.zeros((tq, LANES), F32) for _ in subs]
    for blk in range(n_blk):
        rows = slice(blk * kb, (blk + 1) * kb)
        for c in subs:
            p = jnp.exp2(s_sc[c, :, rows] - m[c])
            for j in range(kb // LANES):
                l_part[c] = l_part[c] + p[:, j * LANES:(j + 1) * LANES]
            acc[c] = acc[c] + jnp.dot(p.astype(BF16), v_ref[rows, :], preferred_element_type=F32)
    outs = [acc[c] / jnp.sum(l_part[c], axis=-1, keepdims=True) for c in subs]
    o = outs[0] - lam * outs[1]
    ms = jnp.mean(o * o, axis=-1, keepdims=True)
    o_ref[...] = (o * lax.rsqrt(ms + EPS) * sg_ref[...]) * out_scale


DA_KEY_BLOCK = 512


def _da_attn(q, k, v, lam4, subln_g, b, s, lambda_init, tq):
    tq = min(tq, s)
    kb = min(DA_KEY_BLOCK, s)
    return pl.pallas_call(
        functools.partial(_da_attn_kernel, lambda_init=lambda_init, kb=kb),
        grid=(b, N_HEADS, s // tq),
        in_specs=[pl.BlockSpec((None, tq, 2 * LANES), lambda bi, h, i: (bi, i, h)),
                  pl.BlockSpec((None, s, 2 * LANES), lambda bi, h, i: (bi, 0, h)),
                  pl.BlockSpec((None, s, LANES), lambda bi, h, i: (bi, 0, h)),
                  pl.BlockSpec((4, DA_DHEAD), lambda bi, h, i: (0, 0)),
                  pl.BlockSpec((1, LANES), lambda bi, h, i: (0, 0))],
        out_specs=pl.BlockSpec((None, tq, LANES), lambda bi, h, i: (bi, i, h)),
        out_shape=jax.ShapeDtypeStruct((b, s, D_MODEL), F32),
        scratch_shapes=[pltpu.VMEM((2, tq, s), F32)],
        compiler_params=_cparams(("parallel", "parallel", "arbitrary")),
        name="da_attn",
    )(q, k, v, lam4, subln_g.reshape(1, LANES))


def _resid_matmul_kernel(x_ref, a_ref, w_ref, out_ref):
    out_ref[...] = x_ref[...] + jnp.dot(a_ref[...].astype(BF16), w_ref[...], preferred_element_type=F32)


def _resid_matmul(x2, a2, w, tm):
    t = x2.shape[0]
    return pl.pallas_call(
        _resid_matmul_kernel,
        grid=(t // tm,),
        in_specs=[pl.BlockSpec((tm, D_MODEL), lambda i: (i, 0)),
                  pl.BlockSpec((tm, D_MODEL), lambda i: (i, 0)),
                  pl.BlockSpec((D_MODEL, D_MODEL), lambda i: (0, 0))],
        out_specs=pl.BlockSpec((tm, D_MODEL), lambda i: (i, 0)),
        out_shape=jax.ShapeDtypeStruct((t, D_MODEL), F32),
        compiler_params=_cparams(("parallel",)),
        name="resid_matmul",
    )(x2, a2, w)


N_EXTRACT = PEER_TOPK + 1
A_ROWS = 24
N_FULL_ROWS = 8


RANK_NONE = 64.0


def _extract_desc(vals, a_sc):
    rank = jnp.full(vals.shape, RANK_NONE, F32)
    for kk in range(N_EXTRACT):
        mx = jnp.max(vals, axis=0, keepdims=True)
        a_sc[kk:kk + 1, :] = mx
        hit = vals == mx
        rank = jnp.where(hit, float(kk), rank)
        vals = jnp.where(hit, -jnp.inf, vals)
    return rank


def _dup_bf16_words(x):
    hi = lax.bitcast_convert_type(x.astype(BF16).astype(F32), jnp.uint32)
    return hi | (hi >> 16)


def _peer_route_kernel(x_ref, g_ref, wq_ref, keys_ref, hnt_ref, cnt_ref, e1z_ref, r2_ref, e2_ref,
                       a1_sc, a2_sc):
    tm = x_ref.shape[0]
    hn = _rms_rows(x_ref[...], g_ref[...])
    hnt_ref[...] = hn.T.astype(BF16)
    q = jnp.dot(hn.astype(BF16), wq_ref[...], preferred_element_type=F32).astype(BF16)
    neg = jnp.full((A_ROWS, tm), -jnp.inf, F32)
    row = lax.broadcasted_iota(jnp.int32, (A_ROWS, tm), 0)
    for h in range(N_HEADS):
        s_t = []
        for p, a_sc in ((0, a1_sc), (1, a2_sc)):
            hp = 2 * h + p
            st = lax.dot_general(keys_ref[hp], q[:, hp * LANES:(hp + 1) * LANES],
                                 (((1,), (1,)), ((), ())), preferred_element_type=F32)
            a_sc[...] = neg
            rank2 = _extract_desc(st, a_sc)
            s_t.append(st)
        a1 = a1_sc[...]
        a2 = a2_sc[...]
        blocks = [a1[0:1, :] + a2]
        blocks += [a1[p:p + 1, :] + a2[0:SUBLANES, :] for p in range(1, N_FULL_ROWS)]
        blocks.append(a1[N_FULL_ROWS:, :] + a2[0:1, :])
        cand = jnp.concatenate(blocks, axis=0)
        cvals = []
        for kk in range(N_EXTRACT):
            mx = jnp.max(cand, axis=0, keepdims=True)
            cvals.append(mx)
            cand = jnp.where(cand == mx, -jnp.inf, cand)
        c0 = cvals[0]
        z = jnp.ones_like(c0)
        for kk in range(1, PEER_TOPK):
            z = z + jnp.exp(cvals[kk] - c0)
        tau = 0.5 * (cvals[PEER_TOPK - 1] + cvals[PEER_TOPK])
        s1, s2 = s_t
        thr = tau - s1
        cnt = jnp.zeros_like(s1)
        for qq in range(N_EXTRACT):
            cnt = cnt + jnp.where(a2[qq:qq + 1, :] >= thr, 1.0, 0.0)
        cnt_ref[h] = _dup_bf16_words(cnt)
        e1z_ref[h] = _dup_bf16_words(jnp.exp(s1 - a1[0:1, :]) * (0.5 / z))
        r2_ref[h] = pltpu.bitcast(rank2.astype(BF16), jnp.uint32)
        e2_ref[h] = pltpu.bitcast(jnp.exp(s2 - a2[0:1, :]).astype(BF16), jnp.uint32)


def _peer_route(x2, norm_g, wq, keys, tm):
    t = x2.shape[0]
    row_spec = pl.BlockSpec((N_HEADS, N_KEYS, tm), lambda i: (0, 0, i))
    pair_spec = pl.BlockSpec((N_HEADS, N_KEYS // 2, tm), lambda i: (0, 0, i))
    row_tab = jax.ShapeDtypeStruct((N_HEADS, N_KEYS, t), jnp.uint32)
    pair_tab = jax.ShapeDtypeStruct((N_HEADS, N_KEYS // 2, t), jnp.uint32)
    return pl.pallas_call(
        _peer_route_kernel,
        grid=(t // tm,),
        in_specs=[pl.BlockSpec((tm, D_MODEL), lambda i: (i, 0)),
                  pl.BlockSpec((1, D_MODEL), lambda i: (0, 0)),
                  pl.BlockSpec(wq.shape, lambda i: (0, 0)),
                  pl.BlockSpec(keys.shape, lambda i: (0, 0, 0))],
        out_specs=[pl.BlockSpec((D_MODEL, tm), lambda i: (0, i)), row_spec, row_spec, pair_spec, pair_spec],
        out_shape=[jax.ShapeDtypeStruct((D_MODEL, t), BF16), row_tab, row_tab, pair_tab, pair_tab],
        scratch_shapes=[pltpu.VMEM((A_ROWS, tm), F32), pltpu.VMEM((A_ROWS, tm), F32)],
        compiler_params=_cparams(("parallel",)),
        name="peer_route",
    )(x2, norm_g.reshape(1, D_MODEL), wq, keys)


PEER_EC = SUBLANES * N_KEYS
PIPE_STAGES = 3
I_GROUP = 2


def _peer_expert_kernel(hnt_ref, u_ref, vt_ref, cnt_ref, e1z_ref, r2_ref, e2_ref, x_ref, out_ref,
                        acc_sc, s0_sc, s1_sc, p0_sc, p1_sc, *, ec, n_chunks):
    g = pl.program_id(1)
    tm = hnt_ref.shape[1]
    n_i = ec // N_KEYS

    @pl.when(g == 0)
    def _():
        acc_sc[...] = jnp.zeros_like(acc_sc)
        s1_sc[...] = jnp.zeros_like(s1_sc)
        p1_sc[...] = jnp.zeros_like(p1_sc)

    def row_bf16(rows, k, ls):
        words = jnp.broadcast_to(rows[k:k + 1, ls], (N_KEYS // 2, LANES))
        return pltpu.bitcast(words, BF16)

    zero = jnp.zeros((), BF16)

    def gate_tile(ci, lt, s_in, p_out):
        ls = slice(lt * LANES, (lt + 1) * LANES)
        for i0 in range(0, n_i, I_GROUP):
            gates = [jnp.zeros((N_KEYS, LANES), BF16) for _ in range(I_GROUP)]
            for h in range(N_HEADS):
                r2t = pltpu.bitcast(r2_ref[h][:, ls], BF16)
                e2t = pltpu.bitcast(e2_ref[h][:, ls], BF16)
                tile0 = pl.multiple_of(ci * n_i + (i0 // SUBLANES) * SUBLANES, SUBLANES)
                cnt8 = cnt_ref[h, pl.ds(tile0, SUBLANES), :]
                e1z8 = e1z_ref[h, pl.ds(tile0, SUBLANES), :]
                for kk in range(I_GROUP):
                    k = i0 % SUBLANES + kk
                    sel = jnp.where(r2t < row_bf16(cnt8, k, ls), e2t, zero)
                    gates[kk] = gates[kk] + sel * row_bf16(e1z8, k, ls)
            for kk in range(I_GROUP):
                r0 = (i0 + kk) * N_KEYS
                s = s_in[r0:r0 + N_KEYS, ls].astype(BF16)
                act2 = s * (1.0 + lax.erf(s * (2.0 ** -0.5)))
                p_out[r0 // 2:(r0 + N_KEYS) // 2, ls] = pltpu.bitcast(gates[kk] * act2, jnp.uint32)

    def pipeline_step(c, u_blk, vt_blk, s_in, s_out, p_in, p_out):
        ci = jnp.clip(c - 1, 0, n_chunks - 1)
        tiles_per_slab = MXU_LANES // LANES
        for sb in range(tm // MXU_LANES):
            cs = slice(sb * MXU_LANES, (sb + 1) * MXU_LANES)
            acc_sc[:, cs] += jnp.dot(vt_blk, pltpu.bitcast(p_in[:, cs], BF16), preferred_element_type=F32)
            gate_tile(ci, sb * tiles_per_slab, s_in, p_out)
            s_out[:, cs] = jnp.dot(u_blk, hnt_ref[:, cs], preferred_element_type=F32)
            for lt in range(sb * tiles_per_slab + 1, (sb + 1) * tiles_per_slab):
                gate_tile(ci, lt, s_in, p_out)

    pipeline_step(2 * g, u_ref[:ec, :], vt_ref[:, :ec], s1_sc, s0_sc, p1_sc, p0_sc)
    pipeline_step(2 * g + 1, u_ref[ec:, :], vt_ref[:, ec:], s0_sc, s1_sc, p0_sc, p1_sc)

    @pl.when(g == pl.num_programs(1) - 1)
    def _():
        out_ref[...] = x_ref[...] + acc_sc[...].T


def _peer_expert(x2, hnt, u, vt, cnt, e1z, r2, e2, tm, ec):
    t = x2.shape[0]
    assert ec % (SUBLANES * N_KEYS) == 0 and (N_EXPERTS // ec) % 2 == 0
    n_chunks = N_EXPERTS // ec
    n_steps = (n_chunks + PIPE_STAGES - 1) // 2
    last_u = n_chunks // 2 - 1
    row_spec = pl.BlockSpec((N_HEADS, N_KEYS, tm), lambda i, g: (0, 0, i))
    pair_spec = pl.BlockSpec((N_HEADS, N_KEYS // 2, tm), lambda i, g: (0, 0, i))
    return pl.pallas_call(
        functools.partial(_peer_expert_kernel, ec=ec, n_chunks=n_chunks),
        grid=(t // tm, n_steps),
        in_specs=[pl.BlockSpec((D_MODEL, tm), lambda i, g: (0, i)),
                  pl.BlockSpec((2 * ec, D_MODEL), lambda i, g: (jnp.minimum(g, last_u), 0)),
                  pl.BlockSpec((D_MODEL, 2 * ec), lambda i, g: (0, jnp.maximum(g - 1, 0))),
                  row_spec, row_spec, pair_spec, pair_spec,
                  pl.BlockSpec((tm, D_MODEL), lambda i, g: (i, 0))],
        out_specs=pl.BlockSpec((tm, D_MODEL), lambda i, g: (i, 0)),
        out_shape=jax.ShapeDtypeStruct((t, D_MODEL), F32),
        scratch_shapes=[pltpu.VMEM((D_MODEL, tm), F32),
                        pltpu.VMEM((ec, tm), F32), pltpu.VMEM((ec, tm), F32),
                        pltpu.VMEM((ec // 2, tm), jnp.uint32), pltpu.VMEM((ec // 2, tm), jnp.uint32)],
        compiler_params=_cparams(("parallel", "arbitrary")),
        name="peer_expert",
    )(hnt, u, vt, cnt, e1z, r2, e2, x2)


def _peer_ffn(x2, norm_g, w_query, sub_keys, u_table, v_table, tm_route, tm_exp, ec):
    keys = sub_keys.reshape(2 * N_HEADS, N_KEYS, LANES).astype(BF16)
    hnt, cnt, e1z, r2, e2 = _peer_route(x2, norm_g, w_query.astype(BF16), keys, tm_route)
    return _peer_expert(x2, hnt, u_table.astype(BF16), v_table.T.astype(BF16), cnt, e1z, r2, e2, tm_exp, ec)


def _mlstm_layout(w_in, b_gates):
    nq = N_HEADS * ML_DQK
    pad_heads = lambda w: jnp.pad(w.reshape(D_MODEL, N_HEADS, ML_DQK),
                                  ((0, 0), (0, 0), (0, LANES - ML_DQK))).reshape(D_MODEL, N_HEADS * LANES)
    wq = pad_heads(w_in[:, :nq])
    wk = pad_heads(w_in[:, nq:2 * nq])
    wv = w_in[:, 2 * nq:2 * nq + D_MODEL]
    wo = w_in[:, 2 * nq + D_MODEL:2 * nq + 2 * D_MODEL]
    wg = w_in[:, 2 * nq + 2 * D_MODEL:]
    pad_g = lambda w: jnp.pad(w, ((0, 0), (0, LANES - 2 * N_HEADS)))
    w_cat = jnp.concatenate([wq, wk, wv, wo, pad_g(wg[:, :2 * N_HEADS]), pad_g(wg[:, 2 * N_HEADS:])],
                            axis=1).astype(BF16)
    bias = jnp.pad(b_gates.reshape(2, 1, 2 * N_HEADS).astype(F32), ((0, 0), (0, 0), (0, LANES - 2 * N_HEADS)))
    return w_cat, bias


def kernel(x, positions, l0_norm_mix, l0_w_in, l0_b_gates, l0_head_norm, l0_w_out, l0_norm_ffn, l0_peer_wq, l0_peer_keys, l0_peer_u, l0_peer_v, l1_norm_mix, l1_w_qkv, l1_q_norm, l1_k_norm, l1_lambda_q1, l1_lambda_k1, l1_lambda_q2, l1_lambda_k2, l1_subln, l1_w_out, l1_norm_ffn, l1_peer_wq, l1_peer_keys, l1_peer_u, l1_peer_v):
    b, s, d = x.shape
    t = b * s
    tm = min(512, t)
    x2 = x.reshape(t, d)

    w_cat, bias = _mlstm_layout(l0_w_in, l0_b_gates)
    segs = [(0, D_MODEL), (D_MODEL, D_MODEL), (2 * D_MODEL, D_MODEL), (3 * D_MODEL, D_MODEL),
            (4 * D_MODEL, LANES), (4 * D_MODEL + LANES, LANES)]
    q, k, v, o, g_f, g_b = _norm_proj(x2, l0_norm_mix, w_cat, segs, [BF16, BF16, BF16, F32, F32, F32], tm)
    gates = jnp.stack([g_f, g_b]).reshape(2, b, s, LANES)
    to3 = lambda a: a.reshape(b, s, a.shape[-1])
    h2 = _mlstm(to3(q), to3(k), to3(v), gates, bias, b, s)
    x2 = _mlstm_out(x2, h2.reshape(2, t, D_MODEL), o, l0_head_norm, l0_w_out.astype(BF16), tm)
    x2 = _peer_ffn(x2, l0_norm_ffn, l0_peer_wq, l0_peer_keys, l0_peer_u, l0_peer_v, min(256, t), tm, PEER_EC)

    lambda_init = 0.8 - 0.6 * math.exp(-0.3 * 1)
    pos = positions.reshape(t, 1).astype(F32)
    qa, ka, va = _da_proj(x2, l1_norm_mix, l1_w_qkv, l1_q_norm, l1_k_norm, pos, tm)
    lam4 = jnp.stack([l1_lambda_q1, l1_lambda_k1, l1_lambda_q2, l1_lambda_k2]).astype(F32)
    att = _da_attn(to3(qa), to3(ka), to3(va), lam4, l1_subln, b, s, lambda_init, 512)
    x2 = _resid_matmul(x2, att.reshape(t, D_MODEL), l1_w_out.astype(BF16), tm)
    x2 = _peer_ffn(x2, l1_norm_ffn, l1_peer_wq, l1_peer_keys, l1_peer_u, l1_peer_v, min(256, t), tm, PEER_EC)
    return x2.reshape(b, s, d)
```

```python
import functools
import math

import jax
import jax.numpy as jnp
from jax import lax
from jax.experimental import pallas as pl
from jax.experimental.pallas import tpu as pltpu

F32 = jnp.float32
BF16 = jnp.bfloat16
HIGHEST = lax.Precision.HIGHEST

D_MODEL = 1024
EPS = 1e-6
LANES = 128
SUBLANES = 8
MXU_LANES = 256
N_HEADS = 8
ML_DQK = 64
ML_CHUNK = 128
ML_M_INIT = -1e30
DA_DHEAD = 64
ROPE_THETA = 500000.0
ROPE_HALF = 8
N_KEYS = 128
N_EXPERTS = N_KEYS * N_KEYS
PEER_TOPK = 16
VMEM_LIMIT_BYTES = 56 * 1024 * 1024


def _cparams(sem):
    return pltpu.CompilerParams(dimension_semantics=sem, vmem_limit_bytes=VMEM_LIMIT_BYTES)


def _rms_rows(x, g):
    ms = jnp.mean(x * x, axis=-1, keepdims=True)
    return x * lax.rsqrt(ms + EPS) * g


def _norm_proj_kernel(x_ref, g_ref, w_ref, *out_refs, segs):
    hn = _rms_rows(x_ref[...], g_ref[...]).astype(BF16)
    for o_ref, (start, width) in zip(out_refs, segs):
        o_ref[...] = jnp.dot(hn, w_ref[:, start:start + width],
                             preferred_element_type=F32).astype(o_ref.dtype)


def _norm_proj(x2, g, w_cat, segs, dtypes, tm):
    t = x2.shape[0]
    n = w_cat.shape[1]
    return pl.pallas_call(
        functools.partial(_norm_proj_kernel, segs=tuple(segs)),
        grid=(t // tm,),
        in_specs=[pl.BlockSpec((tm, D_MODEL), lambda i: (i, 0)),
                  pl.BlockSpec((1, D_MODEL), lambda i: (0, 0)),
                  pl.BlockSpec((D_MODEL, n), lambda i: (0, 0))],
        out_specs=[pl.BlockSpec((tm, w), lambda i: (i, 0)) for (_, w) in segs],
        out_shape=[jax.ShapeDtypeStruct((t, w), dt) for (_, w), dt in zip(segs, dtypes)],
        compiler_params=_cparams(("parallel",)),
        name="norm_proj",
    )(x2, g.reshape(1, D_MODEL), w_cat)


def _mlstm_kernel(q_ref, k_ref, v_ref, g_ref, bias_ref, m_ref, mt_ref, o_ref, ct_sc, ms_sc, *, L):
    c = pl.program_id(2)

    @pl.when(c == 0)
    def _():
        ct_sc[...] = jnp.zeros_like(ct_sc)
        ms_sc[...] = jnp.full_like(ms_sc, ML_M_INIT)

    g = g_ref[...] + bias_ref[...]
    lf = jnp.minimum(g, 0.0) - jnp.log1p(jnp.exp(-jnp.abs(g)))
    msk_f = m_ref[...]
    bcol_all = jnp.dot(msk_f, lf, precision=HIGHEST, preferred_element_type=F32)
    g_t = g.T
    brow_all = jnp.dot(lf.T, mt_ref[...], precision=HIGHEST, preferred_element_type=F32)
    allowed = msk_f > 0.5
    lane = lax.broadcasted_iota(jnp.int32, (L, LANES), 1)
    ones_blk = (lane == 0).astype(BF16)

    for h in range(N_HEADS):
        sl = slice(h * LANES, (h + 1) * LANES)
        q = q_ref[:, sl] * (ML_DQK ** -0.5)
        k = k_ref[:, sl]
        v_aug = jnp.concatenate([v_ref[:, sl], ones_blk], axis=1)
        b_col = bcol_all[:, N_HEADS + h:N_HEADS + h + 1]
        b_row = brow_all[N_HEADS + h:N_HEADS + h + 1, :]
        li_col = g[:, h:h + 1]
        li_row = g_t[h:h + 1, :]
        b_tot = jnp.min(b_row, axis=1, keepdims=True)
        m_prev = ms_sc[h]
        ct_prev = ct_sc[h]

        a_col = b_tot - b_col + li_col
        m_loc = jnp.max(a_col, axis=0, keepdims=True)
        w_col = jnp.exp(a_col - m_loc)

        dmat = jnp.where(allowed, b_col - b_row + li_row, -jnp.inf)
        g_col = b_col + m_prev
        m_t = jnp.maximum(g_col, jnp.max(dmat, axis=1, keepdims=True))
        sqk = lax.dot_general(q, k, (((1,), (1,)), ((), ())), preferred_element_type=F32)
        sqk = sqk * jnp.exp(dmat - m_t)
        inter = jnp.exp(g_col - m_t)
        nd = (jnp.dot(sqk.astype(BF16), v_aug, preferred_element_type=F32)
              + inter * jnp.dot(q, ct_prev.astype(BF16), preferred_element_type=F32))
        den = jnp.maximum(jnp.abs(nd[:, LANES:LANES + 1]), jnp.exp(-m_t))
        o_ref[:, sl] = nd[:, :LANES] / den

        m_new = jnp.maximum(b_tot + m_prev, m_loc)
        sp = jnp.exp(b_tot + m_prev - m_new)
        sc = jnp.exp(m_loc - m_new)
        wv = (w_col * v_aug.astype(F32)).astype(BF16)
        ct_loc = lax.dot_general(k, wv, (((0,), (0,)), ((), ())), preferred_element_type=F32)
        ct_sc[h] = sp * ct_prev + sc * ct_loc
        ms_sc[h] = m_new


def _mlstm(q, k, v, gates, bias, b, s):
    L = min(ML_CHUNK, s)
    nc = s // L
    tri = jnp.tril(jnp.ones((L, L), F32))
    masks = jnp.stack([tri, tri.T])
    masks_t = jnp.stack([tri.T, tri])

    def cidx(d, c):
        return c + d * (nc - 1 - 2 * c)

    seq_spec = pl.BlockSpec((None, L, D_MODEL), lambda bi, d, c: (bi, cidx(d, c), 0))
    return pl.pallas_call(
        functools.partial(_mlstm_kernel, L=L),
        grid=(b, 2, nc),
        in_specs=[seq_spec, seq_spec, seq_spec,
                  pl.BlockSpec((None, None, L, LANES), lambda bi, d, c: (d, bi, cidx(d, c), 0)),
                  pl.BlockSpec((None, 1, LANES), lambda bi, d, c: (d, 0, 0)),
                  pl.BlockSpec((None, L, L), lambda bi, d, c: (d, 0, 0)),
                  pl.BlockSpec((None, L, L), lambda bi, d, c: (d, 0, 0))],
        out_specs=pl.BlockSpec((None, None, L, D_MODEL), lambda bi, d, c: (d, bi, cidx(d, c), 0)),
        out_shape=jax.ShapeDtypeStruct((2, b, s, D_MODEL), F32),
        scratch_shapes=[pltpu.VMEM((N_HEADS, LANES, 2 * LANES), F32),
                        pltpu.VMEM((N_HEADS, 1, 1), F32)],
        compiler_params=_cparams(("parallel", "parallel", "arbitrary")),
        name="mlstm_scan",
    )(q, k, v, gates, bias, masks, masks_t)


def _mlstm_out_kernel(x_ref, h_ref, o_ref, hg_ref, w_ref, out_ref):
    hs = h_ref[0] + h_ref[1]
    parts = []
    for h in range(N_HEADS):
        blk = hs[:, h * LANES:(h + 1) * LANES]
        ms = jnp.mean(blk * blk, axis=-1, keepdims=True)
        parts.append(blk * lax.rsqrt(ms + EPS))
    hn = jnp.concatenate(parts, axis=1) * hg_ref[...] * jax.nn.sigmoid(o_ref[...])
    out_ref[...] = x_ref[...] + jnp.dot(hn.astype(BF16), w_ref[...], preferred_element_type=F32)


def _mlstm_out(x2, h2, o, head_g, w_out, tm):
    t = x2.shape[0]
    return pl.pallas_call(
        _mlstm_out_kernel,
        grid=(t // tm,),
        in_specs=[pl.BlockSpec((tm, D_MODEL), lambda i: (i, 0)),
                  pl.BlockSpec((2, tm, D_MODEL), lambda i: (0, i, 0)),
                  pl.BlockSpec((tm, D_MODEL), lambda i: (i, 0)),
                  pl.BlockSpec((1, D_MODEL), lambda i: (0, 0)),
                  pl.BlockSpec((D_MODEL, D_MODEL), lambda i: (0, 0))],
        out_specs=pl.BlockSpec((tm, D_MODEL), lambda i: (i, 0)),
        out_shape=jax.ShapeDtypeStruct((t, D_MODEL), F32),
        compiler_params=_cparams(("parallel",)),
        name="mlstm_out",
    )(x2, h2, o, head_g.reshape(1, D_MODEL), w_out)


def _da_proj_kernel(x_ref, g_ref, w_ref, pos_ref, inv_ref, wts_ref, gq_ref, gk_ref,
                    cq_ref, ck_ref, q_ref, k_ref, v_ref):
    hn = _rms_rows(x_ref[...], g_ref[...]).astype(BF16)
    ang = pos_ref[...] * inv_ref[...]
    cos = jnp.cos(ang)
    sin = jnp.sin(ang)
    trig_q = cq_ref[0:1, :] * cos + cq_ref[1:2, :] * sin + cq_ref[2:3, :]
    trig_k = ck_ref[0:1, :] * cos + ck_ref[1:2, :] * sin + ck_ref[2:3, :]
    fq = trig_q * gq_ref[...]
    fk = trig_k * gk_ref[...]
    nsub = 2 * N_HEADS
    for u in range(nsub):
        for off, fac, wts, o_ref in ((0, fq, wts_ref[0:1, :], q_ref), (nsub * LANES, fk, wts_ref[1:2, :], k_ref)):
            raw = jnp.dot(hn, w_ref[:, off + u * LANES: off + (u + 1) * LANES], preferred_element_type=F32)
            ms = jnp.sum(raw * raw * wts, axis=-1, keepdims=True) * (1.0 / DA_DHEAD)
            o_ref[:, u * LANES:(u + 1) * LANES] = (raw * lax.rsqrt(ms + EPS) * fac).astype(BF16)
    v_ref[...] = jnp.dot(hn, w_ref[:, 2 * nsub * LANES:], preferred_element_type=F32).astype(BF16)


def _da_layout(w_qkv, q_g, k_g):
    x1 = list(range(0, 8))
    x2 = list(range(8, 16))
    rest = list(range(16, 64))
    q_groups = [(x1, 1, 0, 0), (x1, 1, 0, 0), (x2, 0, -1, 0), (x2, 0, -1, 0),
                (x2, 1, 0, 0), (x2, 1, 0, 0), (x1, 0, 1, 0), (x1, 0, 1, 0)]
    k_groups = [(x1, 1, 0, 0), (x2, 0, -1, 0), (x1, 1, 0, 0), (x2, 0, -1, 0),
                (x2, 1, 0, 0), (x1, 0, 1, 0), (x2, 1, 0, 0), (x1, 0, 1, 0)]

    def tables(groups):
        src, coef = [], []
        for dims, a, b_, c_ in groups:
            src += dims
            coef += [(a, b_, c_)] * len(dims)
        src += rest
        coef += [(0, 0, 1)] * len(rest)
        pad = LANES - len(src)
        valid = [1.0] * len(src) + [0.0] * pad
        src += [0] * pad
        coef += [(0, 0, 0)] * pad
        return jnp.array(src, jnp.int32), jnp.array(coef, F32).T, jnp.array(valid, F32)

    src_q, coef_q, valid_q = tables(q_groups)
    src_k, coef_k, valid_k = tables(k_groups)
    nsub = 2 * N_HEADS
    wq = w_qkv[:, :nsub * DA_DHEAD].reshape(D_MODEL, nsub, DA_DHEAD)
    wk = w_qkv[:, nsub * DA_DHEAD:2 * nsub * DA_DHEAD].reshape(D_MODEL, nsub, DA_DHEAD)
    wv = w_qkv[:, 2 * nsub * DA_DHEAD:]
    wq_p = (wq[:, :, src_q] * valid_q).reshape(D_MODEL, nsub * LANES)
    wk_p = (wk[:, :, src_k] * valid_k).reshape(D_MODEL, nsub * LANES)
    w_cat = jnp.concatenate([wq_p, wk_p, wv], axis=1).astype(BF16)
    lanes = jnp.arange(LANES)
    plain = (lanes >= 64) & (lanes < 112)
    wts_q = ((lanes < 8) | ((lanes >= 16) & (lanes < 24)) | plain).astype(F32)
    wts_k = ((lanes < 16) | plain).astype(F32)
    wts = jnp.concatenate([wts_q[None], wts_k[None], jnp.zeros((6, LANES), F32)], axis=0)
    gq = (q_g[src_q] * valid_q).reshape(1, LANES) * (DA_DHEAD ** -0.5 * math.log2(math.e))
    gk = (k_g[src_k] * valid_k).reshape(1, LANES)
    freq = jnp.where(lanes < 64, lanes % ROPE_HALF, 0)
    inv = ROPE_THETA ** (-freq.astype(F32) * 2.0 / (2 * ROPE_HALF))
    inv = jnp.where(lanes < 64, inv, 0.0).reshape(1, LANES)
    pad8 = lambda a: jnp.concatenate([a, jnp.zeros((5, LANES), F32)], axis=0)
    return w_cat, inv, wts, gq, gk, pad8(coef_q), pad8(coef_k)


def _da_proj(x2, norm_g, w_qkv, q_g, k_g, pos, tm):
    t = x2.shape[0]
    w_cat, inv, wts, gq, gk, cq, ck = _da_layout(w_qkv, q_g, k_g)
    nq = 2 * N_HEADS * LANES
    const = lambda shp: pl.BlockSpec(shp, lambda i: (0,) * len(shp))
    return pl.pallas_call(
        _da_proj_kernel,
        grid=(t // tm,),
        in_specs=[pl.BlockSpec((tm, D_MODEL), lambda i: (i, 0)),
                  const((1, D_MODEL)), const(w_cat.shape),
                  pl.BlockSpec((tm, 1), lambda i: (i, 0)),
                  const((1, LANES)), const((8, LANES)), const((1, LANES)), const((1, LANES)),
                  const((8, LANES)), const((8, LANES))],
        out_specs=[pl.BlockSpec((tm, nq), lambda i: (i, 0)),
                   pl.BlockSpec((tm, nq), lambda i: (i, 0)),
                   pl.BlockSpec((tm, D_MODEL), lambda i: (i, 0))],
        out_shape=[jax.ShapeDtypeStruct((t, nq), BF16),
                   jax.ShapeDtypeStruct((t, nq), BF16),
                   jax.ShapeDtypeStruct((t, D_MODEL), BF16)],
        compiler_params=_cparams(("parallel",)),
        name="da_proj",
    )(x2, norm_g.reshape(1, D_MODEL), w_cat, pos, inv, wts, gq, gk, cq, ck)


def _da_attn_kernel(q_ref, k_ref, v_ref, lam_ref, sg_ref, o_ref, s_sc, *, lambda_init, kb):
    out_scale = 1.0 - lambda_init
    lam_v = lam_ref[...]
    lam = (jnp.exp(jnp.sum(lam_v[0:1] * lam_v[1:2], axis=-1, keepdims=True))
           - jnp.exp(jnp.sum(lam_v[2:3] * lam_v[3:4], axis=-1, keepdims=True))
           + lambda_init)
    tq = q_ref.shape[0]
    th = tq // 2
    n_blk = k_ref.shape[0] // kb
    subs = (0, 1)

    ones_blk = (lax.broadcasted_iota(jnp.int32, (kb, LANES), 1) == 0).astype(BF16)

    def score_block(qs, r0, blk, m_part):
        rows = slice(blk * kb, (blk + 1) * kb)
        for c in subs:
            s = lax.dot_general(qs[c], k_ref[rows, c * LANES:(c + 1) * LANES], (((1,), (1,)), ((), ())),
                                preferred_element_type=F32)
            s_sc[c, r0:r0 + th, rows] = s
            for j in range(kb // LANES):
                m_part[c] = jnp.maximum(m_part[c], s[:, j * LANES:(j + 1) * LANES])

    def attend_block(m, r0, blk, acc):
        rows = slice(blk * kb, (blk + 1) * kb)
        v_aug = jnp.concatenate([v_ref[rows, :], ones_blk], axis=1)
        for c in subs:
            p = jnp.exp2(s_sc[c, r0:r0 + th, rows] - m[c])
            acc[c] = acc[c] + jnp.dot(p.astype(BF16), v_aug, preferred_element_type=F32)

    def finish(r0, acc):
        outs = [acc[c][:, :LANES] / acc[c][:, LANES:LANES + 1] for c in subs]
        o = outs[0] - lam * outs[1]
        ms = jnp.mean(o * o, axis=-1, keepdims=True)
        o_ref[r0:r0 + th, :] = (o * lax.rsqrt(ms + EPS) * sg_ref[...]) * out_scale

    new_max = lambda: [jnp.full((th, LANES), -jnp.inf, F32) for _ in subs]
    new_acc = lambda: [jnp.zeros((th, 2 * LANES), F32) for _ in subs]
    row_max = lambda m_part: [jnp.max(m_part[c], axis=-1, keepdims=True) for c in subs]

    q_a = [q_ref[:th, c * LANES:(c + 1) * LANES] for c in subs]
    mp_a, acc_a = new_max(), new_acc()
    for blk in range(n_blk):
        score_block(q_a, 0, blk, mp_a)
    m_a = row_max(mp_a)
    bits = lax.bitcast_convert_type(m_a[0] + m_a[1], jnp.uint32)
    zero = ((bits >> 16) >> 16).astype(F32)
    q_b = [(q_ref[th:, c * LANES:(c + 1) * LANES].astype(F32) + zero).astype(BF16) for c in subs]
    mp_b, acc_b = new_max(), new_acc()
    for blk in range(n_blk):
        score_block(q_b, th, blk, mp_b)
        attend_block(m_a, 0, blk, acc_a)
    finish(0, acc_a)
    m_b = row_max(mp_b)
    for blk in range(n_blk):
        attend_block(m_b, th, blk, acc_b)
    finish(th, acc_b)


DA_KEY_BLOCK = 512


def _da_attn(q, k, v, lam4, subln_g, b, s, lambda_init, tq):
    tq = min(tq, s)
    kb = min(DA_KEY_BLOCK, s)
    return pl.pallas_call(
        functools.partial(_da_attn_kernel, lambda_init=lambda_init, kb=kb),
        grid=(b, N_HEADS, s // tq),
        in_specs=[pl.BlockSpec((None, tq, 2 * LANES), lambda bi, h, i: (bi, i, h)),
                  pl.BlockSpec((None, s, 2 * LANES), lambda bi, h, i: (bi, 0, h)),
                  pl.BlockSpec((None, s, LANES), lambda bi, h, i: (bi, 0, h)),
                  pl.BlockSpec((4, DA_DHEAD), lambda bi, h, i: (0, 0)),
                  pl.BlockSpec((1, LANES), lambda bi, h, i: (0, 0))],
        out_specs=pl.BlockSpec((None, tq, LANES), lambda bi, h, i: (bi, i, h)),
        out_shape=jax.ShapeDtypeStruct((b, s, D_MODEL), F32),
        scratch_shapes=[pltpu.VMEM((2, tq, s), F32)],
        compiler_params=_cparams(("parallel", "parallel", "arbitrary")),
        name="da_attn",
    )(q, k, v, lam4, subln_g.reshape(1, LANES))


def _resid_matmul_kernel(x_ref, a_ref, w_ref, out_ref):
    out_ref[...] = x_ref[...] + jnp.dot(a_ref[...].astype(BF16), w_ref[...], preferred_element_type=F32)


def _resid_matmul(x2, a2, w, tm):
    t = x2.shape[0]
    return pl.pallas_call(
        _resid_matmul_kernel,
        grid=(t // tm,),
        in_specs=[pl.BlockSpec((tm, D_MODEL), lambda i: (i, 0)),
                  pl.BlockSpec((tm, D_MODEL), lambda i: (i, 0)),
                  pl.BlockSpec((D_MODEL, D_MODEL), lambda i: (0, 0))],
        out_specs=pl.BlockSpec((tm, D_MODEL), lambda i: (i, 0)),
        out_shape=jax.ShapeDtypeStruct((t, D_MODEL), F32),
        compiler_params=_cparams(("parallel",)),
        name="resid_matmul",
    )(x2, a2, w)


N_EXTRACT = PEER_TOPK + 1
A_ROWS = 24
N_FULL_ROWS = 8


RANK_NONE = 64.0


def _extract_desc(vals, a_sc):
    rank = jnp.full(vals.shape, RANK_NONE, F32)
    for kk in range(N_EXTRACT):
        mx = jnp.max(vals, axis=0, keepdims=True)
        a_sc[kk:kk + 1, :] = mx
        hit = vals == mx
        rank = jnp.where(hit, float(kk), rank)
        vals = jnp.where(hit, -jnp.inf, vals)
    return rank


def _dup_bf16_words(x):
    hi = lax.bitcast_convert_type(x.astype(BF16).astype(F32), jnp.uint32)
    return hi | (hi >> 16)


def _peer_route_kernel(x_ref, g_ref, wq_ref, keys_ref, hnt_ref, cnt_ref, e1z_ref, r2_ref, e2_ref,
                       a1_sc, a2_sc):
    tm = x_ref.shape[0]
    hn = _rms_rows(x_ref[...], g_ref[...])
    hnt_ref[...] = hn.T.astype(BF16)
    q = jnp.dot(hn.astype(BF16), wq_ref[...], preferred_element_type=F32).astype(BF16)
    neg = jnp.full((A_ROWS, tm), -jnp.inf, F32)
    row = lax.broadcasted_iota(jnp.int32, (A_ROWS, tm), 0)
    for h in range(N_HEADS):
        s_t = []
        for p, a_sc in ((0, a1_sc), (1, a2_sc)):
            hp = 2 * h + p
            st = lax.dot_general(keys_ref[hp], q[:, hp * LANES:(hp + 1) * LANES],
                                 (((1,), (1,)), ((), ())), preferred_element_type=F32)
            a_sc[...] = neg
            rank2 = _extract_desc(st, a_sc)
            s_t.append(st)
        a1 = a1_sc[...]
        a2 = a2_sc[...]
        blocks = [a1[0:1, :] + a2]
        blocks += [a1[p:p + 1, :] + a2[0:SUBLANES, :] for p in range(1, N_FULL_ROWS)]
        blocks.append(a1[N_FULL_ROWS:, :] + a2[0:1, :])
        cand = jnp.concatenate(blocks, axis=0)
        cvals = []
        for kk in range(N_EXTRACT):
            mx = jnp.max(cand, axis=0, keepdims=True)
            cvals.append(mx)
            cand = jnp.where(cand == mx, -jnp.inf, cand)
        c0 = cvals[0]
        z = jnp.ones_like(c0)
        for kk in range(1, PEER_TOPK):
            z = z + jnp.exp(cvals[kk] - c0)
        tau = 0.5 * (cvals[PEER_TOPK - 1] + cvals[PEER_TOPK])
        s1, s2 = s_t
        thr = tau - s1
        cnt = jnp.zeros_like(s1)
        for qq in range(N_EXTRACT):
            cnt = cnt + jnp.where(a2[qq:qq + 1, :] >= thr, 1.0, 0.0)
        cnt_ref[h] = _dup_bf16_words(cnt)
        e1z_ref[h] = _dup_bf16_words(jnp.exp(s1 - a1[0:1, :]) * (0.5 / z))
        r2_ref[h] = pltpu.bitcast(rank2.astype(BF16), jnp.uint32)
        e2_ref[h] = pltpu.bitcast(jnp.exp(s2 - a2[0:1, :]).astype(BF16), jnp.uint32)


def _peer_route(x2, norm_g, wq, keys, tm):
    t = x2.shape[0]
    row_spec = pl.BlockSpec((N_HEADS, N_KEYS, tm), lambda i: (0, 0, i))
    pair_spec = pl.BlockSpec((N_HEADS, N_KEYS // 2, tm), lambda i: (0, 0, i))
    row_tab = jax.ShapeDtypeStruct((N_HEADS, N_KEYS, t), jnp.uint32)
    pair_tab = jax.ShapeDtypeStruct((N_HEADS, N_KEYS // 2, t), jnp.uint32)
    return pl.pallas_call(
        _peer_route_kernel,
        grid=(t // tm,),
        in_specs=[pl.BlockSpec((tm, D_MODEL), lambda i: (i, 0)),
                  pl.BlockSpec((1, D_MODEL), lambda i: (0, 0)),
                  pl.BlockSpec(wq.shape, lambda i: (0, 0)),
                  pl.BlockSpec(keys.shape, lambda i: (0, 0, 0))],
        out_specs=[pl.BlockSpec((D_MODEL, tm), lambda i: (0, i)), row_spec, row_spec, pair_spec, pair_spec],
        out_shape=[jax.ShapeDtypeStruct((D_MODEL, t), BF16), row_tab, row_tab, pair_tab, pair_tab],
        scratch_shapes=[pltpu.VMEM((A_ROWS, tm), F32), pltpu.VMEM((A_ROWS, tm), F32)],
        compiler_params=_cparams(("parallel",)),
        name="peer_route",
    )(x2, norm_g.reshape(1, D_MODEL), wq, keys)


PEER_EC = SUBLANES * N_KEYS
PIPE_STAGES = 3
I_GROUP = 2


def _peer_expert_kernel(hnt_ref, u_ref, vt_ref, cnt_ref, e1z_ref, r2_ref, e2_ref, x_ref, out_ref,
                        acc_sc, s0_sc, s1_sc, p0_sc, p1_sc, *, ec):
    g = pl.program_id(1)
    tm = hnt_ref.shape[1]
    n_i = ec // N_KEYS

    def row_bf16(rows, k, ls):
        words = jnp.broadcast_to(rows[k:k + 1, ls], (N_KEYS // 2, LANES))
        return pltpu.bitcast(words, BF16)

    zero = jnp.zeros((), BF16)

    def gate_tile(ci, lt, s_in, p_out):
        ls = slice(lt * LANES, (lt + 1) * LANES)
        for i0 in range(0, n_i, I_GROUP):
            gates = [jnp.zeros((N_KEYS, LANES), BF16) for _ in range(I_GROUP)]
            for h in range(N_HEADS):
                r2t = pltpu.bitcast(r2_ref[h][:, ls], BF16)
                e2t = pltpu.bitcast(e2_ref[h][:, ls], BF16)
                tile0 = pl.multiple_of(ci * n_i + (i0 // SUBLANES) * SUBLANES, SUBLANES)
                cnt8 = cnt_ref[h, pl.ds(tile0, SUBLANES), :]
                e1z8 = e1z_ref[h, pl.ds(tile0, SUBLANES), :]
                for kk in range(I_GROUP):
                    k = i0 % SUBLANES + kk
                    sel = jnp.where(r2t < row_bf16(cnt8, k, ls), e2t, zero)
                    gates[kk] = gates[kk] + sel * row_bf16(e1z8, k, ls)
            for kk in range(I_GROUP):
                r0 = (i0 + kk) * N_KEYS
                s = s_in[r0:r0 + N_KEYS, ls].astype(BF16)
                act2 = s * (1.0 + lax.erf(s * (2.0 ** -0.5)))
                p_out[r0 // 2:(r0 + N_KEYS) // 2, ls] = pltpu.bitcast(gates[kk] * act2, jnp.uint32)

    def pipeline_step(c, u_blk, vt_blk, s_in, s_out, p_in, p_out, run_out, run_gate, run_pre):
        tiles_per_slab = MXU_LANES // LANES
        for sb in range(tm // MXU_LANES):
            cs = slice(sb * MXU_LANES, (sb + 1) * MXU_LANES)
            if run_out:
                acc_sc[:, cs] += jnp.dot(vt_blk, pltpu.bitcast(p_in[:, cs], BF16),
                                         preferred_element_type=F32)
            if run_gate:
                gate_tile(c - 1, sb * tiles_per_slab, s_in, p_out)
            if run_pre:
                s_out[:, cs] = jnp.dot(u_blk, hnt_ref[:, cs], preferred_element_type=F32)
            if run_gate:
                for lt in range(sb * tiles_per_slab + 1, (sb + 1) * tiles_per_slab):
                    gate_tile(c - 1, lt, s_in, p_out)

    def grid_step(first, last):
        even = (u_ref[:ec, :], vt_ref[:, :ec], s1_sc, s0_sc, p1_sc, p0_sc)
        odd = (u_ref[ec:, :], vt_ref[:, ec:], s0_sc, s1_sc, p0_sc, p1_sc)
        pipeline_step(2 * g, *even, run_out=not first, run_gate=not first, run_pre=not last)
        pipeline_step(2 * g + 1, *odd, run_out=not first, run_gate=not last, run_pre=not last)

    last_g = pl.num_programs(1) - 1

    @pl.when(g == 0)
    def _():
        acc_sc[...] = jnp.zeros_like(acc_sc)
        grid_step(True, False)

    @pl.when((g > 0) & (g < last_g))
    def _():
        grid_step(False, False)

    @pl.when(g == last_g)
    def _():
        grid_step(False, True)
        out_ref[...] = x_ref[...] + acc_sc[...].T


def _peer_expert(x2, hnt, u, vt, cnt, e1z, r2, e2, tm, ec):
    t = x2.shape[0]
    assert ec % (SUBLANES * N_KEYS) == 0 and (N_EXPERTS // ec) % 2 == 0
    n_chunks = N_EXPERTS // ec
    n_steps = (n_chunks + PIPE_STAGES - 1) // 2
    last_u = n_chunks // 2 - 1
    row_spec = pl.BlockSpec((N_HEADS, N_KEYS, tm), lambda i, g: (0, 0, i))
    pair_spec = pl.BlockSpec((N_HEADS, N_KEYS // 2, tm), lambda i, g: (0, 0, i))
    return pl.pallas_call(
        functools.partial(_peer_expert_kernel, ec=ec),
        grid=(t // tm, n_steps),
        in_specs=[pl.BlockSpec((D_MODEL, tm), lambda i, g: (0, i)),
                  pl.BlockSpec((2 * ec, D_MODEL), lambda i, g: (jnp.minimum(g, last_u), 0)),
                  pl.BlockSpec((D_MODEL, 2 * ec), lambda i, g: (0, jnp.maximum(g - 1, 0))),
                  row_spec, row_spec, pair_spec, pair_spec,
                  pl.BlockSpec((tm, D_MODEL), lambda i, g: (i, 0))],
        out_specs=pl.BlockSpec((tm, D_MODEL), lambda i, g: (i, 0)),
        out_shape=jax.ShapeDtypeStruct((t, D_MODEL), F32),
        scratch_shapes=[pltpu.VMEM((D_MODEL, tm), F32),
                        pltpu.VMEM((ec, tm), F32), pltpu.VMEM((ec, tm), F32),
                        pltpu.VMEM((ec // 2, tm), jnp.uint32), pltpu.VMEM((ec // 2, tm), jnp.uint32)],
        compiler_params=_cparams(("parallel", "arbitrary")),
        name="peer_expert",
    )(hnt, u, vt, cnt, e1z, r2, e2, x2)


def _peer_ffn(x2, norm_g, w_query, sub_keys, u_table, v_table, tm_route, tm_exp, ec):
    keys = sub_keys.reshape(2 * N_HEADS, N_KEYS, LANES).astype(BF16)
    hnt, cnt, e1z, r2, e2 = _peer_route(x2, norm_g, w_query.astype(BF16), keys, tm_route)
    return _peer_expert(x2, hnt, u_table.astype(BF16), v_table.T.astype(BF16), cnt, e1z, r2, e2, tm_exp, ec)


def _mlstm_layout(w_in, b_gates):
    nq = N_HEADS * ML_DQK
    pad_heads = lambda w: jnp.pad(w.reshape(D_MODEL, N_HEADS, ML_DQK),
                                  ((0, 0), (0, 0), (0, LANES - ML_DQK))).reshape(D_MODEL, N_HEADS * LANES)
    wq = pad_heads(w_in[:, :nq])
    wk = pad_heads(w_in[:, nq:2 * nq])
    wv = w_in[:, 2 * nq:2 * nq + D_MODEL]
    wo = w_in[:, 2 * nq + D_MODEL:2 * nq + 2 * D_MODEL]
    wg = w_in[:, 2 * nq + 2 * D_MODEL:]
    pad_g = lambda w: jnp.pad(w, ((0, 0), (0, LANES - 2 * N_HEADS)))
    w_cat = jnp.concatenate([wq, wk, wv, wo, pad_g(wg[:, :2 * N_HEADS]), pad_g(wg[:, 2 * N_HEADS:])],
                            axis=1).astype(BF16)
    bias = jnp.pad(b_gates.reshape(2, 1, 2 * N_HEADS).astype(F32), ((0, 0), (0, 0), (0, LANES - 2 * N_HEADS)))
    return w_cat, bias


def kernel(x, positions, l0_norm_mix, l0_w_in, l0_b_gates, l0_head_norm, l0_w_out, l0_norm_ffn, l0_peer_wq, l0_peer_keys, l0_peer_u, l0_peer_v, l1_norm_mix, l1_w_qkv, l1_q_norm, l1_k_norm, l1_lambda_q1, l1_lambda_k1, l1_lambda_q2, l1_lambda_k2, l1_subln, l1_w_out, l1_norm_ffn, l1_peer_wq, l1_peer_keys, l1_peer_u, l1_peer_v):
    b, s, d = x.shape
    t = b * s
    tm = min(512, t)
    x2 = x.reshape(t, d)

    w_cat, bias = _mlstm_layout(l0_w_in, l0_b_gates)
    segs = [(0, D_MODEL), (D_MODEL, D_MODEL), (2 * D_MODEL, D_MODEL), (3 * D_MODEL, D_MODEL),
            (4 * D_MODEL, LANES), (4 * D_MODEL + LANES, LANES)]
    q, k, v, o, g_f, g_b = _norm_proj(x2, l0_norm_mix, w_cat, segs, [BF16, BF16, BF16, F32, F32, F32], tm)
    gates = jnp.stack([g_f, g_b]).reshape(2, b, s, LANES)
    to3 = lambda a: a.reshape(b, s, a.shape[-1])
    h2 = _mlstm(to3(q), to3(k), to3(v), gates, bias, b, s)
    x2 = _mlstm_out(x2, h2.reshape(2, t, D_MODEL), o, l0_head_norm, l0_w_out.astype(BF16), tm)
    x2 = _peer_ffn(x2, l0_norm_ffn, l0_peer_wq, l0_peer_keys, l0_peer_u, l0_peer_v, min(256, t), tm, PEER_EC)

    lambda_init = 0.8 - 0.6 * math.exp(-0.3 * 1)
    pos = positions.reshape(t, 1).astype(F32)
    qa, ka, va = _da_proj(x2, l1_norm_mix, l1_w_qkv, l1_q_norm, l1_k_norm, pos, tm)
    lam4 = jnp.stack([l1_lambda_q1, l1_lambda_k1, l1_lambda_q2, l1_lambda_k2]).astype(F32)
    att = _da_attn(to3(qa), to3(ka), to3(va), lam4, l1_subln, b, s, lambda_init, 512)
    x2 = _resid_matmul(x2, att.reshape(t, D_MODEL), l1_w_out.astype(BF16), tm)
    x2 = _peer_ffn(x2, l1_norm_ffn, l1_peer_wq, l1_peer_keys, l1_peer_u, l1_peer_v, min(256, t), tm, PEER_EC)
    return x2.reshape(b, s, d)
```

```python
import functools
import math

import jax
import jax.numpy as jnp
from jax import lax
from jax.experimental import pallas as pl
from jax.experimental.pallas import tpu as pltpu

F32 = jnp.float32
BF16 = jnp.bfloat16
HIGHEST = lax.Precision.HIGHEST

D_MODEL = 1024
EPS = 1e-6
LANES = 128
SUBLANES = 8
MXU_LANES = 256
N_HEADS = 8
ML_DQK = 64
ML_CHUNK = 256
ML_M_INIT = -1e30
DA_DHEAD = 64
ROPE_THETA = 500000.0
ROPE_HALF = 8
N_KEYS = 128
N_EXPERTS = N_KEYS * N_KEYS
PEER_TOPK = 16
VMEM_LIMIT_BYTES = 56 * 1024 * 1024


def _cparams(sem):
    return pltpu.CompilerParams(dimension_semantics=sem, vmem_limit_bytes=VMEM_LIMIT_BYTES)


def _rms_rows(x, g):
    ms = jnp.mean(x * x, axis=-1, keepdims=True)
    return x * lax.rsqrt(ms + EPS) * g


def _norm_proj_kernel(x_ref, g_ref, w_ref, *out_refs, segs):
    hn = _rms_rows(x_ref[...], g_ref[...]).astype(BF16)
    for o_ref, (start, width) in zip(out_refs, segs):
        o_ref[...] = jnp.dot(hn, w_ref[:, start:start + width],
                             preferred_element_type=F32).astype(o_ref.dtype)


def _norm_proj(x2, g, w_cat, segs, dtypes, tm):
    t = x2.shape[0]
    n = w_cat.shape[1]
    return pl.pallas_call(
        functools.partial(_norm_proj_kernel, segs=tuple(segs)),
        grid=(t // tm,),
        in_specs=[pl.BlockSpec((tm, D_MODEL), lambda i: (i, 0)),
                  pl.BlockSpec((1, D_MODEL), lambda i: (0, 0)),
                  pl.BlockSpec((D_MODEL, n), lambda i: (0, 0))],
        out_specs=[pl.BlockSpec((tm, w), lambda i: (i, 0)) for (_, w) in segs],
        out_shape=[jax.ShapeDtypeStruct((t, w), dt) for (_, w), dt in zip(segs, dtypes)],
        compiler_params=_cparams(("parallel",)),
        name="norm_proj",
    )(x2, g.reshape(1, D_MODEL), w_cat)


def _mlstm_kernel(q_ref, k_ref, v_ref, g_ref, bias_ref, m_ref, mt_ref, o_ref, ct_sc, ms_sc, *, L):
    c = pl.program_id(2)

    @pl.when(c == 0)
    def _():
        ct_sc[...] = jnp.zeros_like(ct_sc)
        ms_sc[...] = jnp.full_like(ms_sc, ML_M_INIT)

    g = g_ref[...] + bias_ref[...]
    lf = jnp.minimum(g, 0.0) - jnp.log1p(jnp.exp(-jnp.abs(g)))
    msk_f = m_ref[...]
    bcol_all = jnp.dot(msk_f, lf, precision=HIGHEST, preferred_element_type=F32)
    g_t = g.T
    brow_all = jnp.dot(lf.T, mt_ref[...], precision=HIGHEST, preferred_element_type=F32)
    allowed = msk_f > 0.5
    lane = lax.broadcasted_iota(jnp.int32, (L, LANES), 1)
    ones_blk = (lane == 0).astype(BF16)

    for h in range(N_HEADS):
        sl = slice(h * LANES, (h + 1) * LANES)
        q = q_ref[:, sl] * (ML_DQK ** -0.5)
        k = k_ref[:, sl]
        v_aug = jnp.concatenate([v_ref[:, sl], ones_blk], axis=1)
        b_col = bcol_all[:, N_HEADS + h:N_HEADS + h + 1]
        b_row = brow_all[N_HEADS + h:N_HEADS + h + 1, :]
        li_col = g[:, h:h + 1]
        li_row = g_t[h:h + 1, :]
        b_tot = jnp.min(b_row, axis=1, keepdims=True)
        m_prev = ms_sc[h]
        ct_prev = ct_sc[h]

        a_col = b_tot - b_col + li_col
        m_loc = jnp.max(a_col, axis=0, keepdims=True)
        w_col = jnp.exp(a_col - m_loc)

        dmat = jnp.where(allowed, b_col - b_row + li_row, -jnp.inf)
        g_col = b_col + m_prev
        m_t = jnp.maximum(g_col, jnp.max(dmat, axis=1, keepdims=True))
        sqk = lax.dot_general(q, k, (((1,), (1,)), ((), ())), preferred_element_type=F32)
        sqk = sqk * jnp.exp(dmat - m_t)
        inter = jnp.exp(g_col - m_t)
        nd = (jnp.dot(sqk.astype(BF16), v_aug, preferred_element_type=F32)
              + inter * jnp.dot(q, ct_prev.astype(BF16), preferred_element_type=F32))
        den = jnp.maximum(jnp.abs(nd[:, LANES:LANES + 1]), jnp.exp(-m_t))
        o_ref[:, sl] = nd[:, :LANES] / den

        m_new = jnp.maximum(b_tot + m_prev, m_loc)
        sp = jnp.exp(b_tot + m_prev - m_new)
        sc = jnp.exp(m_loc - m_new)
        wv = (w_col * v_aug.astype(F32)).astype(BF16)
        ct_loc = lax.dot_general(k, wv, (((0,), (0,)), ((), ())), preferred_element_type=F32)
        ct_sc[h] = sp * ct_prev + sc * ct_loc
        ms_sc[h] = m_new


def _mlstm(q, k, v, gates, bias, b, s):
    L = min(ML_CHUNK, s)
    nc = s // L
    tri = jnp.tril(jnp.ones((L, L), F32))
    masks = jnp.stack([tri, tri.T])
    masks_t = jnp.stack([tri.T, tri])

    def cidx(d, c):
        return c + d * (nc - 1 - 2 * c)

    seq_spec = pl.BlockSpec((None, L, D_MODEL), lambda bi, d, c: (bi, cidx(d, c), 0))
    return pl.pallas_call(
        functools.partial(_mlstm_kernel, L=L),
        grid=(b, 2, nc),
        in_specs=[seq_spec, seq_spec, seq_spec,
                  pl.BlockSpec((None, None, L, LANES), lambda bi, d, c: (d, bi, cidx(d, c), 0)),
                  pl.BlockSpec((None, 1, LANES), lambda bi, d, c: (d, 0, 0)),
                  pl.BlockSpec((None, L, L), lambda bi, d, c: (d, 0, 0)),
                  pl.BlockSpec((None, L, L), lambda bi, d, c: (d, 0, 0))],
        out_specs=pl.BlockSpec((None, None, L, D_MODEL), lambda bi, d, c: (d, bi, cidx(d, c), 0)),
        out_shape=jax.ShapeDtypeStruct((2, b, s, D_MODEL), F32),
        scratch_shapes=[pltpu.VMEM((N_HEADS, LANES, 2 * LANES), F32),
                        pltpu.VMEM((N_HEADS, 1, 1), F32)],
        compiler_params=_cparams(("parallel", "parallel", "arbitrary")),
        name="mlstm_scan",
    )(q, k, v, gates, bias, masks, masks_t)


def _mlstm_out_kernel(x_ref, h_ref, o_ref, hg_ref, w_ref, out_ref):
    hs = h_ref[0] + h_ref[1]
    parts = []
    for h in range(N_HEADS):
        blk = hs[:, h * LANES:(h + 1) * LANES]
        ms = jnp.mean(blk * blk, axis=-1, keepdims=True)
        parts.append(blk * lax.rsqrt(ms + EPS))
    hn = jnp.concatenate(parts, axis=1) * hg_ref[...] * jax.nn.sigmoid(o_ref[...])
    out_ref[...] = x_ref[...] + jnp.dot(hn.astype(BF16), w_ref[...], preferred_element_type=F32)


def _mlstm_out(x2, h2, o, head_g, w_out, tm):
    t = x2.shape[0]
    return pl.pallas_call(
        _mlstm_out_kernel,
        grid=(t // tm,),
        in_specs=[pl.BlockSpec((tm, D_MODEL), lambda i: (i, 0)),
                  pl.BlockSpec((2, tm, D_MODEL), lambda i: (0, i, 0)),
                  pl.BlockSpec((tm, D_MODEL), lambda i: (i, 0)),
                  pl.BlockSpec((1, D_MODEL), lambda i: (0, 0)),
                  pl.BlockSpec((D_MODEL, D_MODEL), lambda i: (0, 0))],
        out_specs=pl.BlockSpec((tm, D_MODEL), lambda i: (i, 0)),
        out_shape=jax.ShapeDtypeStruct((t, D_MODEL), F32),
        compiler_params=_cparams(("parallel",)),
        name="mlstm_out",
    )(x2, h2, o, head_g.reshape(1, D_MODEL), w_out)


def _da_proj_kernel(x_ref, g_ref, w_ref, pos_ref, inv_ref, wts_ref, gq_ref, gk_ref,
                    cq_ref, ck_ref, q_ref, k_ref, v_ref):
    hn = _rms_rows(x_ref[...], g_ref[...]).astype(BF16)
    ang = pos_ref[...] * inv_ref[...]
    cos = jnp.cos(ang)
    sin = jnp.sin(ang)
    trig_q = cq_ref[0:1, :] * cos + cq_ref[1:2, :] * sin + cq_ref[2:3, :]
    trig_k = ck_ref[0:1, :] * cos + ck_ref[1:2, :] * sin + ck_ref[2:3, :]
    fq = trig_q * gq_ref[...]
    fk = trig_k * gk_ref[...]
    nsub = 2 * N_HEADS
    per_dot = MXU_LANES // LANES
    for u0 in range(0, nsub, per_dot):
        for off, fac, wts, o_ref in ((0, fq, wts_ref[0:1, :], q_ref), (nsub * LANES, fk, wts_ref[1:2, :], k_ref)):
            raw2 = jnp.dot(hn, w_ref[:, off + u0 * LANES: off + (u0 + per_dot) * LANES],
                           preferred_element_type=F32)
            for du in range(per_dot):
                raw = raw2[:, du * LANES:(du + 1) * LANES]
                ms = jnp.sum(raw * raw * wts, axis=-1, keepdims=True) * (1.0 / DA_DHEAD)
                o_ref[:, (u0 + du) * LANES:(u0 + du + 1) * LANES] = (raw * lax.rsqrt(ms + EPS) * fac).astype(BF16)
    v_ref[...] = jnp.dot(hn, w_ref[:, 2 * nsub * LANES:], preferred_element_type=F32).astype(BF16)


def _da_layout(w_qkv, q_g, k_g):
    x1 = list(range(0, 8))
    x2 = list(range(8, 16))
    rest = list(range(16, 64))
    q_groups = [(x1, 1, 0, 0), (x1, 1, 0, 0), (x2, 0, -1, 0), (x2, 0, -1, 0),
                (x2, 1, 0, 0), (x2, 1, 0, 0), (x1, 0, 1, 0), (x1, 0, 1, 0)]
    k_groups = [(x1, 1, 0, 0), (x2, 0, -1, 0), (x1, 1, 0, 0), (x2, 0, -1, 0),
                (x2, 1, 0, 0), (x1, 0, 1, 0), (x2, 1, 0, 0), (x1, 0, 1, 0)]

    def tables(groups):
        src, coef = [], []
        for dims, a, b_, c_ in groups:
            src += dims
            coef += [(a, b_, c_)] * len(dims)
        src += rest
        coef += [(0, 0, 1)] * len(rest)
        pad = LANES - len(src)
        valid = [1.0] * len(src) + [0.0] * pad
        src += [0] * pad
        coef += [(0, 0, 0)] * pad
        return jnp.array(src, jnp.int32), jnp.array(coef, F32).T, jnp.array(valid, F32)

    src_q, coef_q, valid_q = tables(q_groups)
    src_k, coef_k, valid_k = tables(k_groups)
    nsub = 2 * N_HEADS
    wq = w_qkv[:, :nsub * DA_DHEAD].reshape(D_MODEL, nsub, DA_DHEAD)
    wk = w_qkv[:, nsub * DA_DHEAD:2 * nsub * DA_DHEAD].reshape(D_MODEL, nsub, DA_DHEAD)
    wv = w_qkv[:, 2 * nsub * DA_DHEAD:]
    wq_p = (wq[:, :, src_q] * valid_q).reshape(D_MODEL, nsub * LANES)
    wk_p = (wk[:, :, src_k] * valid_k).reshape(D_MODEL, nsub * LANES)
    w_cat = jnp.concatenate([wq_p, wk_p, wv], axis=1).astype(BF16)
    lanes = jnp.arange(LANES)
    plain = (lanes >= 64) & (lanes < 112)
    wts_q = ((lanes < 8) | ((lanes >= 16) & (lanes < 24)) | plain).astype(F32)
    wts_k = ((lanes < 16) | plain).astype(F32)
    wts = jnp.concatenate([wts_q[None], wts_k[None], jnp.zeros((6, LANES), F32)], axis=0)
    gq = (q_g[src_q] * valid_q).reshape(1, LANES) * (DA_DHEAD ** -0.5 * math.log2(math.e))
    gk = (k_g[src_k] * valid_k).reshape(1, LANES)
    freq = jnp.where(lanes < 64, lanes % ROPE_HALF, 0)
    inv = ROPE_THETA ** (-freq.astype(F32) * 2.0 / (2 * ROPE_HALF))
    inv = jnp.where(lanes < 64, inv, 0.0).reshape(1, LANES)
    pad8 = lambda a: jnp.concatenate([a, jnp.zeros((5, LANES), F32)], axis=0)
    return w_cat, inv, wts, gq, gk, pad8(coef_q), pad8(coef_k)


def _da_proj(x2, norm_g, w_qkv, q_g, k_g, pos, tm):
    t = x2.shape[0]
    w_cat, inv, wts, gq, gk, cq, ck = _da_layout(w_qkv, q_g, k_g)
    nq = 2 * N_HEADS * LANES
    const = lambda shp: pl.BlockSpec(shp, lambda i: (0,) * len(shp))
    return pl.pallas_call(
        _da_proj_kernel,
        grid=(t // tm,),
        in_specs=[pl.BlockSpec((tm, D_MODEL), lambda i: (i, 0)),
                  const((1, D_MODEL)), const(w_cat.shape),
                  pl.BlockSpec((tm, 1), lambda i: (i, 0)),
                  const((1, LANES)), const((8, LANES)), const((1, LANES)), const((1, LANES)),
                  const((8, LANES)), const((8, LANES))],
        out_specs=[pl.BlockSpec((tm, nq), lambda i: (i, 0)),
                   pl.BlockSpec((tm, nq), lambda i: (i, 0)),
                   pl.BlockSpec((tm, D_MODEL), lambda i: (i, 0))],
        out_shape=[jax.ShapeDtypeStruct((t, nq), BF16),
                   jax.ShapeDtypeStruct((t, nq), BF16),
                   jax.ShapeDtypeStruct((t, D_MODEL), BF16)],
        compiler_params=_cparams(("parallel",)),
        name="da_proj",
    )(x2, norm_g.reshape(1, D_MODEL), w_cat, pos, inv, wts, gq, gk, cq, ck)


def _da_attn_kernel(q_ref, k_ref, v_ref, lam_ref, sg_ref, o_ref, s_sc, *, lambda_init, kb):
    out_scale = 1.0 - lambda_init
    lam_v = lam_ref[...]
    lam = (jnp.exp(jnp.sum(lam_v[0:1] * lam_v[1:2], axis=-1, keepdims=True))
           - jnp.exp(jnp.sum(lam_v[2:3] * lam_v[3:4], axis=-1, keepdims=True))
           + lambda_init)
    tq = q_ref.shape[0]
    th = tq // 2
    n_blk = k_ref.shape[0] // kb
    subs = (0, 1)

    ones_blk = (lax.broadcasted_iota(jnp.int32, (kb, LANES), 1) == 0).astype(BF16)

    def score_block(qs, r0, blk, m_part):
        rows = slice(blk * kb, (blk + 1) * kb)
        for c in subs:
            s = lax.dot_general(qs[c], k_ref[rows, c * LANES:(c + 1) * LANES], (((1,), (1,)), ((), ())),
                                preferred_element_type=F32)
            s_sc[c, r0:r0 + th, rows] = s
            for j in range(kb // LANES):
                m_part[c] = jnp.maximum(m_part[c], s[:, j * LANES:(j + 1) * LANES])

    def attend_block(m, r0, blk, acc):
        rows = slice(blk * kb, (blk + 1) * kb)
        v_aug = jnp.concatenate([v_ref[rows, :], ones_blk], axis=1)
        for c in subs:
            p = jnp.exp2(s_sc[c, r0:r0 + th, rows] - m[c])
            acc[c] = acc[c] + jnp.dot(p.astype(BF16), v_aug, preferred_element_type=F32)

    def finish(r0, acc):
        outs = [acc[c][:, :LANES] / acc[c][:, LANES:LANES + 1] for c in subs]
        o = outs[0] - lam * outs[1]
        ms = jnp.mean(o * o, axis=-1, keepdims=True)
        o_ref[r0:r0 + th, :] = (o * lax.rsqrt(ms + EPS) * sg_ref[...]) * out_scale

    new_max = lambda: [jnp.full((th, LANES), -jnp.inf, F32) for _ in subs]
    new_acc = lambda: [jnp.zeros((th, 2 * LANES), F32) for _ in subs]
    row_max = lambda m_part: [jnp.max(m_part[c], axis=-1, keepdims=True) for c in subs]

    q_a = [q_ref[:th, c * LANES:(c + 1) * LANES] for c in subs]
    mp_a, acc_a = new_max(), new_acc()
    for blk in range(n_blk):
        score_block(q_a, 0, blk, mp_a)
    m_a = row_max(mp_a)
    bits = lax.bitcast_convert_type(m_a[0] + m_a[1], jnp.uint32)
    zero = ((bits >> 16) >> 16).astype(F32)
    q_b = [(q_ref[th:, c * LANES:(c + 1) * LANES].astype(F32) + zero).astype(BF16) for c in subs]
    mp_b, acc_b = new_max(), new_acc()
    for blk in range(n_blk):
        score_block(q_b, th, blk, mp_b)
        attend_block(m_a, 0, blk, acc_a)
    finish(0, acc_a)
    m_b = row_max(mp_b)
    for blk in range(n_blk):
        attend_block(m_b, th, blk, acc_b)
    finish(th, acc_b)


DA_KEY_BLOCK = 512


def _da_attn(q, k, v, lam4, subln_g, b, s, lambda_init, tq):
    tq = min(tq, s)
    kb = min(DA_KEY_BLOCK, s)
    return pl.pallas_call(
        functools.partial(_da_attn_kernel, lambda_init=lambda_init, kb=kb),
        grid=(b, N_HEADS, s // tq),
        in_specs=[pl.BlockSpec((None, tq, 2 * LANES), lambda bi, h, i: (bi, i, h)),
                  pl.BlockSpec((None, s, 2 * LANES), lambda bi, h, i: (bi, 0, h)),
                  pl.BlockSpec((None, s, LANES), lambda bi, h, i: (bi, 0, h)),
                  pl.BlockSpec((4, DA_DHEAD), lambda bi, h, i: (0, 0)),
                  pl.BlockSpec((1, LANES), lambda bi, h, i: (0, 0))],
        out_specs=pl.BlockSpec((None, tq, LANES), lambda bi, h, i: (bi, i, h)),
        out_shape=jax.ShapeDtypeStruct((b, s, D_MODEL), F32),
        scratch_shapes=[pltpu.VMEM((2, tq, s), F32)],
        compiler_params=_cparams(("parallel", "parallel", "arbitrary")),
        name="da_attn",
    )(q, k, v, lam4, subln_g.reshape(1, LANES))


def _resid_matmul_kernel(x_ref, a_ref, w_ref, out_ref):
    out_ref[...] = x_ref[...] + jnp.dot(a_ref[...].astype(BF16), w_ref[...], preferred_element_type=F32)


def _resid_matmul(x2, a2, w, tm):
    t = x2.shape[0]
    return pl.pallas_call(
        _resid_matmul_kernel,
        grid=(t // tm,),
        in_specs=[pl.BlockSpec((tm, D_MODEL), lambda i: (i, 0)),
                  pl.BlockSpec((tm, D_MODEL), lambda i: (i, 0)),
                  pl.BlockSpec((D_MODEL, D_MODEL), lambda i: (0, 0))],
        out_specs=pl.BlockSpec((tm, D_MODEL), lambda i: (i, 0)),
        out_shape=jax.ShapeDtypeStruct((t, D_MODEL), F32),
        compiler_params=_cparams(("parallel",)),
        name="resid_matmul",
    )(x2, a2, w)


N_EXTRACT = PEER_TOPK + 1
A_ROWS = 24
N_FULL_ROWS = 8


RANK_NONE = 64.0


def _oddeven_merge_sort_pairs(n):
    pairs = []

    def merge(lo, m, r):
        step = 2 * r
        if step < m:
            merge(lo, m, step)
            merge(lo + r, m, step)
            pairs.extend((i, i + r) for i in range(lo + r, lo + m - r, step))
        else:
            pairs.append((lo, lo + r))

    def sort(lo, m):
        if m > 1:
            sort(lo, m // 2)
            sort(lo + m // 2, m // 2)
            merge(lo, m, 1)

    sort(0, n)
    return pairs


def _extract_desc(vals, a_sc):
    n_grp = vals.shape[0] // SUBLANES
    for lt in range(vals.shape[1] // LANES):
        ls = slice(lt * LANES, (lt + 1) * LANES)
        col = [vals[v * SUBLANES:(v + 1) * SUBLANES, ls] for v in range(n_grp)]
        for i, j in _oddeven_merge_sort_pairs(n_grp):
            col[i], col[j] = jnp.maximum(col[i], col[j]), jnp.minimum(col[i], col[j])
        for t in range(N_EXTRACT):
            mx = jnp.max(col[0], axis=0, keepdims=True)
            a_sc[t:t + 1, ls] = mx
            remaining = N_EXTRACT - 1 - t
            if remaining:
                hit = col[0] == mx
                for v in range(min(remaining, n_grp)):
                    below = col[v + 1] if v + 1 < n_grp else -jnp.inf
                    col[v] = jnp.where(hit, below, col[v])


def _ranks_of(vals, a):
    rank = jnp.full(vals.shape, RANK_NONE, F32)
    for qq in range(N_EXTRACT):
        rank = jnp.where(vals == a[qq:qq + 1, :], float(qq), rank)
    return rank


def _dup_bf16_words(x):
    hi = lax.bitcast_convert_type(x.astype(BF16).astype(F32), jnp.uint32)
    return hi | (hi >> 16)


def _peer_route_kernel(x_ref, g_ref, wq_ref, keys_ref, hnt_ref, cnt_ref, e1z_ref, r2_ref, e2_ref,
                       a1_sc, a2_sc, c_sc):
    tm = x_ref.shape[0]
    hn = _rms_rows(x_ref[...], g_ref[...])
    hnt_ref[...] = hn.T.astype(BF16)
    q = jnp.dot(hn.astype(BF16), wq_ref[...], preferred_element_type=F32).astype(BF16)
    neg = jnp.full((A_ROWS, tm), -jnp.inf, F32)
    for h in range(N_HEADS):
        s_t = []
        for p, a_sc in ((0, a1_sc), (1, a2_sc)):
            hp = 2 * h + p
            st = lax.dot_general(keys_ref[hp], q[:, hp * LANES:(hp + 1) * LANES],
                                 (((1,), (1,)), ((), ())), preferred_element_type=F32)
            a_sc[...] = neg
            _extract_desc(st, a_sc)
            s_t.append(st)
        a1 = a1_sc[...]
        a2 = a2_sc[...]
        rank2 = _ranks_of(s_t[1], a2)
        blocks = [a1[0:1, :] + a2]
        blocks += [a1[p:p + 1, :] + a2[0:SUBLANES, :] for p in range(1, N_FULL_ROWS)]
        blocks.append(a1[N_FULL_ROWS:, :] + a2[0:1, :])
        blocks.append(jnp.full((N_KEYS - sum(blk.shape[0] for blk in blocks), tm), -jnp.inf, F32))
        c_sc[...] = neg
        _extract_desc(jnp.concatenate(blocks, axis=0), c_sc)
        cv = c_sc[...]
        topk_rows = lax.broadcasted_iota(jnp.int32, (A_ROWS, tm), 0) < PEER_TOPK
        z = jnp.sum(jnp.where(topk_rows, jnp.exp(cv - cv[0:1, :]), 0.0), axis=0, keepdims=True)
        tau = 0.5 * (cv[PEER_TOPK - 1:PEER_TOPK, :] + cv[PEER_TOPK:PEER_TOPK + 1, :])
        s1, s2 = s_t
        thr_rank = tau - a1
        cnt_rank = jnp.zeros_like(a1)
        for qq in range(N_EXTRACT):
            cnt_rank = cnt_rank + jnp.where(a2[qq:qq + 1, :] >= thr_rank, 1.0, 0.0)
        cnt = jnp.zeros_like(s1)
        for pp in range(N_EXTRACT):
            cnt = jnp.where(s1 == a1[pp:pp + 1, :], cnt_rank[pp:pp + 1, :], cnt)
        cnt_ref[h] = _dup_bf16_words(cnt)
        e1z_ref[h] = _dup_bf16_words(jnp.exp(s1 - a1[0:1, :]) * (0.5 / z))
        r2_ref[h] = pltpu.bitcast(rank2.astype(BF16), jnp.uint32)
        e2_ref[h] = pltpu.bitcast(jnp.exp(s2 - a2[0:1, :]).astype(BF16), jnp.uint32)


def _peer_route(x2, norm_g, wq, keys, tm):
    t = x2.shape[0]
    row_spec = pl.BlockSpec((N_HEADS, N_KEYS, tm), lambda i: (0, 0, i))
    pair_spec = pl.BlockSpec((N_HEADS, N_KEYS // 2, tm), lambda i: (0, 0, i))
    row_tab = jax.ShapeDtypeStruct((N_HEADS, N_KEYS, t), jnp.uint32)
    pair_tab = jax.ShapeDtypeStruct((N_HEADS, N_KEYS // 2, t), jnp.uint32)
    return pl.pallas_call(
        _peer_route_kernel,
        grid=(t // tm,),
        in_specs=[pl.BlockSpec((tm, D_MODEL), lambda i: (i, 0)),
                  pl.BlockSpec((1, D_MODEL), lambda i: (0, 0)),
                  pl.BlockSpec(wq.shape, lambda i: (0, 0)),
                  pl.BlockSpec(keys.shape, lambda i: (0, 0, 0))],
        out_specs=[pl.BlockSpec((D_MODEL, tm), lambda i: (0, i)), row_spec, row_spec, pair_spec, pair_spec],
        out_shape=[jax.ShapeDtypeStruct((D_MODEL, t), BF16), row_tab, row_tab, pair_tab, pair_tab],
        scratch_shapes=[pltpu.VMEM((A_ROWS, tm), F32) for _ in range(3)],
        compiler_params=_cparams(("parallel",)),
        name="peer_route",
    )(x2, norm_g.reshape(1, D_MODEL), wq, keys)


PEER_EC = SUBLANES * N_KEYS
PIPE_STAGES = 3
I_GROUP = 2


def _peer_expert_kernel(hnt_ref, u_ref, vt_ref, cnt_ref, e1z_ref, r2_ref, e2_ref, x_ref, out_ref,
                        acc_sc, s0_sc, s1_sc, p0_sc, p1_sc, *, ec):
    g = pl.program_id(1)
    tm = hnt_ref.shape[1]
    n_i = ec // N_KEYS

    def row_bf16(rows, k, ls):
        words = jnp.broadcast_to(rows[k:k + 1, ls], (N_KEYS // 2, LANES))
        return pltpu.bitcast(words, BF16)

    zero = jnp.zeros((), BF16)

    def gate_tile(ci, lt, s_in, p_out):
        ls = slice(lt * LANES, (lt + 1) * LANES)
        for i0 in range(0, n_i, I_GROUP):
            gates = [jnp.zeros((N_KEYS, LANES), BF16) for _ in range(I_GROUP)]
            for h in range(N_HEADS):
                r2t = pltpu.bitcast(r2_ref[h][:, ls], BF16)
                e2t = pltpu.bitcast(e2_ref[h][:, ls], BF16)
                tile0 = pl.multiple_of(ci * n_i + (i0 // SUBLANES) * SUBLANES, SUBLANES)
                cnt8 = cnt_ref[h, pl.ds(tile0, SUBLANES), :]
                e1z8 = e1z_ref[h, pl.ds(tile0, SUBLANES), :]
                for kk in range(I_GROUP):
                    k = i0 % SUBLANES + kk
                    sel = jnp.where(r2t < row_bf16(cnt8, k, ls), e2t, zero)
                    gates[kk] = gates[kk] + sel * row_bf16(e1z8, k, ls)
            for kk in range(I_GROUP):
                r0 = (i0 + kk) * N_KEYS
                s = s_in[r0:r0 + N_KEYS, ls].astype(BF16)
                act2 = s * (1.0 + lax.erf(s * (2.0 ** -0.5)))
                p_out[r0 // 2:(r0 + N_KEYS) // 2, ls] = pltpu.bitcast(gates[kk] * act2, jnp.uint32)

    def pipeline_step(c, u_blk, vt_blk, s_in, s_out, p_in, p_out, run_out, run_gate, run_pre):
        tiles_per_slab = MXU_LANES // LANES
        for sb in range(tm // MXU_LANES):
            cs = slice(sb * MXU_LANES, (sb + 1) * MXU_LANES)
            if run_out:
                acc_sc[:, cs] += jnp.dot(vt_blk, pltpu.bitcast(p_in[:, cs], BF16),
                                         preferred_element_type=F32)
            if run_gate:
                gate_tile(c - 1, sb * tiles_per_slab, s_in, p_out)
            if run_pre:
                s_out[:, cs] = jnp.dot(u_blk, hnt_ref[:, cs], preferred_element_type=F32)
            if run_gate:
                for lt in range(sb * tiles_per_slab + 1, (sb + 1) * tiles_per_slab):
                    gate_tile(c - 1, lt, s_in, p_out)

    def grid_step(first, last):
        even = (u_ref[:ec, :], vt_ref[:, :ec], s1_sc, s0_sc, p1_sc, p0_sc)
        odd = (u_ref[ec:, :], vt_ref[:, ec:], s0_sc, s1_sc, p0_sc, p1_sc)
        pipeline_step(2 * g, *even, run_out=not first, run_gate=not first, run_pre=not last)
        pipeline_step(2 * g + 1, *odd, run_out=not first, run_gate=not last, run_pre=not last)

    last_g = pl.num_programs(1) - 1

    @pl.when(g == 0)
    def _():
        acc_sc[...] = jnp.zeros_like(acc_sc)
        grid_step(True, False)

    @pl.when((g > 0) & (g < last_g))
    def _():
        grid_step(False, False)

    @pl.when(g == last_g)
    def _():
        grid_step(False, True)
        out_ref[...] = x_ref[...] + acc_sc[...].T


def _peer_expert(x2, hnt, u, vt, cnt, e1z, r2, e2, tm, ec):
    t = x2.shape[0]
    assert ec % (SUBLANES * N_KEYS) == 0 and (N_EXPERTS // ec) % 2 == 0
    n_chunks = N_EXPERTS // ec
    n_steps = (n_chunks + PIPE_STAGES - 1) // 2
    last_u = n_chunks // 2 - 1
    row_spec = pl.BlockSpec((N_HEADS, N_KEYS, tm), lambda i, g: (0, 0, i))
    pair_spec = pl.BlockSpec((N_HEADS, N_KEYS // 2, tm), lambda i, g: (0, 0, i))
    return pl.pallas_call(
        functools.partial(_peer_expert_kernel, ec=ec),
        grid=(t // tm, n_steps),
        in_specs=[pl.BlockSpec((D_MODEL, tm), lambda i, g: (0, i)),
                  pl.BlockSpec((2 * ec, D_MODEL), lambda i, g: (jnp.minimum(g, last_u), 0)),
                  pl.BlockSpec((D_MODEL, 2 * ec), lambda i, g: (0, jnp.maximum(g - 1, 0))),
                  row_spec, row_spec, pair_spec, pair_spec,
                  pl.BlockSpec((tm, D_MODEL), lambda i, g: (i, 0))],
        out_specs=pl.BlockSpec((tm, D_MODEL), lambda i, g: (i, 0)),
        out_shape=jax.ShapeDtypeStruct((t, D_MODEL), F32),
        scratch_shapes=[pltpu.VMEM((D_MODEL, tm), F32),
                        pltpu.VMEM((ec, tm), F32), pltpu.VMEM((ec, tm), F32),
                        pltpu.VMEM((ec // 2, tm), jnp.uint32), pltpu.VMEM((ec // 2, tm), jnp.uint32)],
        compiler_params=_cparams(("parallel", "arbitrary")),
        name="peer_expert",
    )(hnt, u, vt, cnt, e1z, r2, e2, x2)


def _peer_ffn(x2, norm_g, w_query, sub_keys, u_table, v_table, tm_route, tm_exp, ec):
    keys = sub_keys.reshape(2 * N_HEADS, N_KEYS, LANES).astype(BF16)
    hnt, cnt, e1z, r2, e2 = _peer_route(x2, norm_g, w_query.astype(BF16), keys, tm_route)
    return _peer_expert(x2, hnt, u_table.astype(BF16), v_table.T.astype(BF16), cnt, e1z, r2, e2, tm_exp, ec)


def _mlstm_layout(w_in, b_gates):
    nq = N_HEADS * ML_DQK
    pad_heads = lambda w: jnp.pad(w.reshape(D_MODEL, N_HEADS, ML_DQK),
                                  ((0, 0), (0, 0), (0, LANES - ML_DQK))).reshape(D_MODEL, N_HEADS * LANES)
    wq = pad_heads(w_in[:, :nq])
    wk = pad_heads(w_in[:, nq:2 * nq])
    wv = w_in[:, 2 * nq:2 * nq + D_MODEL]
    wo = w_in[:, 2 * nq + D_MODEL:2 * nq + 2 * D_MODEL]
    wg = w_in[:, 2 * nq + 2 * D_MODEL:]
    pad_g = lambda w: jnp.pad(w, ((0, 0), (0, LANES - 2 * N_HEADS)))
    w_cat = jnp.concatenate([wq, wk, wv, wo, pad_g(wg[:, :2 * N_HEADS]), pad_g(wg[:, 2 * N_HEADS:])],
                            axis=1).astype(BF16)
    bias = jnp.pad(b_gates.reshape(2, 1, 2 * N_HEADS).astype(F32), ((0, 0), (0, 0), (0, LANES - 2 * N_HEADS)))
    return w_cat, bias


def kernel(x, positions, l0_norm_mix, l0_w_in, l0_b_gates, l0_head_norm, l0_w_out, l0_norm_ffn, l0_peer_wq, l0_peer_keys, l0_peer_u, l0_peer_v, l1_norm_mix, l1_w_qkv, l1_q_norm, l1_k_norm, l1_lambda_q1, l1_lambda_k1, l1_lambda_q2, l1_lambda_k2, l1_subln, l1_w_out, l1_norm_ffn, l1_peer_wq, l1_peer_keys, l1_peer_u, l1_peer_v):
    b, s, d = x.shape
    t = b * s
    tm = min(512, t)
    x2 = x.reshape(t, d)

    w_cat, bias = _mlstm_layout(l0_w_in, l0_b_gates)
    segs = [(0, D_MODEL), (D_MODEL, D_MODEL), (2 * D_MODEL, D_MODEL), (3 * D_MODEL, D_MODEL),
            (4 * D_MODEL, LANES), (4 * D_MODEL + LANES, LANES)]
    q, k, v, o, g_f, g_b = _norm_proj(x2, l0_norm_mix, w_cat, segs, [BF16, BF16, BF16, F32, F32, F32], tm)
    gates = jnp.stack([g_f, g_b]).reshape(2, b, s, LANES)
    to3 = lambda a: a.reshape(b, s, a.shape[-1])
    h2 = _mlstm(to3(q), to3(k), to3(v), gates, bias, b, s)
    x2 = _mlstm_out(x2, h2.reshape(2, t, D_MODEL), o, l0_head_norm, l0_w_out.astype(BF16), tm)
    x2 = _peer_ffn(x2, l0_norm_ffn, l0_peer_wq, l0_peer_keys, l0_peer_u, l0_peer_v, min(256, t), tm, PEER_EC)

    lambda_init = 0.8 - 0.6 * math.exp(-0.3 * 1)
    pos = positions.reshape(t, 1).astype(F32)
    qa, ka, va = _da_proj(x2, l1_norm_mix, l1_w_qkv, l1_q_norm, l1_k_norm, pos, tm)
    lam4 = jnp.stack([l1_lambda_q1, l1_lambda_k1, l1_lambda_q2, l1_lambda_k2]).astype(F32)
    att = _da_attn(to3(qa), to3(ka), to3(va), lam4, l1_subln, b, s, lambda_init, 512)
    x2 = _resid_matmul(x2, att.reshape(t, D_MODEL), l1_w_out.astype(BF16), tm)
    x2 = _peer_ffn(x2, l1_norm_ffn, l1_peer_wq, l1_peer_keys, l1_peer_u, l1_peer_v, min(256, t), tm, PEER_EC)
    return x2.reshape(b, s, d)
```

```python
import functools
import math

import jax
import jax.numpy as jnp
from jax import lax
from jax.experimental import pallas as pl
from jax.experimental.pallas import tpu as pltpu

F32 = jnp.float32
BF16 = jnp.bfloat16
HIGHEST = lax.Precision.HIGHEST

D_MODEL = 1024
EPS = 1e-6
LANES = 128
SUBLANES = 8
MXU_LANES = 256
N_HEADS = 8
ML_DQK = 64
ML_CHUNK = 256
ML_M_INIT = -1e30
DA_DHEAD = 64
ROPE_THETA = 500000.0
ROPE_HALF = 8
N_KEYS = 128
N_EXPERTS = N_KEYS * N_KEYS
PEER_TOPK = 16
VMEM_LIMIT_BYTES = 56 * 1024 * 1024


def _cparams(sem):
    return pltpu.CompilerParams(dimension_semantics=sem, vmem_limit_bytes=VMEM_LIMIT_BYTES)


def _rms_rows(x, g):
    ms = jnp.mean(x * x, axis=-1, keepdims=True)
    return x * lax.rsqrt(ms + EPS) * g


N_GATES = 4 * N_HEADS


def _mlstm_proj_kernel(x_ref, g_ref, wk_ref, wt_ref, k_ref, qt_ref, vt_ref, ot_ref, gt_ref):
    hn = _rms_rows(x_ref[...], g_ref[...]).astype(BF16)
    k_ref[...] = jnp.dot(hn, wk_ref[...], preferred_element_type=F32).astype(BF16)

    def feature_major(r0, n):
        return lax.dot_general(wt_ref[r0:r0 + n, :], hn, (((1,), (1,)), ((), ())), preferred_element_type=F32)

    qt_ref[...] = (feature_major(0, D_MODEL) * (ML_DQK ** -0.5)).astype(BF16)
    vt_ref[...] = feature_major(D_MODEL, D_MODEL).astype(BF16)
    ot_ref[...] = feature_major(2 * D_MODEL, D_MODEL)
    gt_ref[...] = feature_major(3 * D_MODEL, N_GATES)


def _mlstm_proj(x2, g, w_k, w_rows, tm):
    t = x2.shape[0]
    fm = lambda n, dt: (pl.BlockSpec((n, tm), lambda i: (0, i)), jax.ShapeDtypeStruct((n, t), dt))
    outs = [(pl.BlockSpec((tm, D_MODEL), lambda i: (i, 0)), jax.ShapeDtypeStruct((t, D_MODEL), BF16)),
            fm(D_MODEL, BF16), fm(D_MODEL, BF16), fm(D_MODEL, F32), fm(N_GATES, F32)]
    return pl.pallas_call(
        _mlstm_proj_kernel,
        grid=(t // tm,),
        in_specs=[pl.BlockSpec((tm, D_MODEL), lambda i: (i, 0)),
                  pl.BlockSpec((1, D_MODEL), lambda i: (0, 0)),
                  pl.BlockSpec(w_k.shape, lambda i: (0, 0)),
                  pl.BlockSpec(w_rows.shape, lambda i: (0, 0))],
        out_specs=[o[0] for o in outs],
        out_shape=[o[1] for o in outs],
        compiler_params=_cparams(("parallel",)),
        name="mlstm_proj",
    )(x2, g.reshape(1, D_MODEL), w_k, w_rows)


def _mlstm_kernel(k_ref, qt_ref, vt_ref, gt_ref, bias_ref, mt_ref, o_ref, ct_sc, m_sc, *, L):
    d = pl.program_id(1)
    c = pl.program_id(2)

    @pl.when(c == 0)
    def _():
        ct_sc[...] = jnp.zeros_like(ct_sc)
        m_sc[...] = jnp.full_like(m_sc, ML_M_INIT)

    row0 = pl.multiple_of(d * (2 * N_HEADS), 2 * N_HEADS)
    g = gt_ref[pl.ds(row0, 2 * N_HEADS), :] + bias_ref[pl.ds(row0, 2 * N_HEADS), :]
    li = g[:N_HEADS]
    gf = g[N_HEADS:]
    lf = jnp.minimum(gf, 0.0) - jnp.log1p(jnp.exp(-jnp.abs(gf)))
    mask_f = mt_ref[...]
    b = jnp.dot(lf, mask_f, precision=HIGHEST, preferred_element_type=F32)
    b_tot = jnp.min(b, axis=1, keepdims=True)
    r = li - b
    a = b_tot + r
    m_loc = jnp.max(a, axis=1, keepdims=True)
    w = jnp.exp(a - m_loc)
    m_prev = m_sc[:, 0:1]
    g_in = b + m_prev
    allowed = mask_f > 0.5
    r_cols = jnp.concatenate([r, jnp.zeros((LANES - N_HEADS, L), F32)], axis=0).T

    def masked_r(h):
        return jnp.where(allowed, r_cols[:, h:h + 1], -jnp.inf)

    c_max = jnp.concatenate([jnp.max(masked_r(h), axis=0, keepdims=True) for h in range(N_HEADS)], axis=0)
    m_t = jnp.maximum(g_in, b + c_max)
    u = b - m_t
    inter = jnp.exp(g_in - m_t)
    e_neg = jnp.exp(-m_t)
    m_new = jnp.maximum(b_tot + m_prev, m_loc)
    s_prev = jnp.exp(b_tot + m_prev - m_new)
    s_loc = jnp.exp(m_loc - m_new)
    ones_rows = (lax.broadcasted_iota(jnp.int32, (LANES, L), 0) == 0).astype(BF16)

    for h in range(N_HEADS):
        rows = slice(h * LANES, (h + 1) * LANES)
        hh = slice(h, h + 1)
        qt = qt_ref[rows, :]
        k = k_ref[:, rows]
        v_aug = jnp.concatenate([vt_ref[rows, :], ones_rows], axis=0)
        ct_prev = ct_sc[h]
        e_t = jnp.exp(masked_r(h) + u[hh, :])
        p_t = (jnp.dot(k, qt, preferred_element_type=F32) * e_t).astype(BF16)
        nd = (jnp.dot(v_aug, p_t, preferred_element_type=F32)
              + inter[hh, :] * jnp.dot(ct_prev.astype(BF16), qt, preferred_element_type=F32))
        den = jnp.maximum(jnp.abs(nd[LANES:LANES + 1, :]), e_neg[hh, :])
        o_ref[rows, :] = nd[:LANES, :] / den
        wv = (v_aug.astype(F32) * w[hh, :]).astype(BF16)
        ct_loc = jnp.dot(wv, k, preferred_element_type=F32)
        ct_sc[h] = s_prev[hh, :] * ct_prev + s_loc[hh, :] * ct_loc
    m_sc[...] = jnp.broadcast_to(m_new, m_sc.shape)


def _mlstm(k, qt, vt, gt, bias, b, s):
    L = min(ML_CHUNK, s)
    nc = s // L
    tri = jnp.tril(jnp.ones((L, L), F32))
    masks_t = jnp.stack([tri.T, tri])
    bias_b = jnp.broadcast_to(bias.reshape(N_GATES, 1).astype(F32), (N_GATES, L))

    def col(bi, d, c):
        return bi * nc + c + d * (nc - 1 - 2 * c)

    fm = lambda n: pl.BlockSpec((n, L), lambda bi, d, c: (0, col(bi, d, c)))
    return pl.pallas_call(
        functools.partial(_mlstm_kernel, L=L),
        grid=(b, 2, nc),
        in_specs=[pl.BlockSpec((L, D_MODEL), lambda bi, d, c: (col(bi, d, c), 0)),
                  fm(D_MODEL), fm(D_MODEL), fm(N_GATES),
                  pl.BlockSpec((N_GATES, L), lambda bi, d, c: (0, 0)),
                  pl.BlockSpec((None, L, L), lambda bi, d, c: (d, 0, 0))],
        out_specs=pl.BlockSpec((None, D_MODEL, L), lambda bi, d, c: (d, 0, col(bi, d, c))),
        out_shape=jax.ShapeDtypeStruct((2, D_MODEL, b * s), F32),
        scratch_shapes=[pltpu.VMEM((N_HEADS, 2 * LANES, LANES), F32),
                        pltpu.VMEM((N_HEADS, LANES), F32)],
        compiler_params=_cparams(("parallel", "parallel", "arbitrary")),
        name="mlstm_scan",
    )(k, qt, vt, gt, bias_b, masks_t)


def _mlstm_out_kernel(x_ref, h_ref, ot_ref, hg_ref, w_ref, out_ref):
    tm = x_ref.shape[0]
    hs = h_ref[0] + h_ref[1]
    parts = []
    for h in range(N_HEADS):
        blk = hs[h * LANES:(h + 1) * LANES, :]
        ms = jnp.mean(blk * blk, axis=0, keepdims=True)
        parts.append(blk * lax.rsqrt(ms + EPS))
    gain = jnp.concatenate([hg_ref[...]] * (tm // LANES), axis=1)
    hn_t = (jnp.concatenate(parts, axis=0) * gain * jax.nn.sigmoid(ot_ref[...])).astype(BF16)
    out_ref[...] = x_ref[...] + lax.dot_general(hn_t, w_ref[...], (((0,), (0,)), ((), ())),
                                                preferred_element_type=F32)


def _mlstm_out(x2, h2, ot, head_g, w_out, tm):
    t = x2.shape[0]
    gain = jnp.broadcast_to(head_g.reshape(D_MODEL, 1).astype(F32), (D_MODEL, LANES))
    return pl.pallas_call(
        _mlstm_out_kernel,
        grid=(t // tm,),
        in_specs=[pl.BlockSpec((tm, D_MODEL), lambda i: (i, 0)),
                  pl.BlockSpec((2, D_MODEL, tm), lambda i: (0, 0, i)),
                  pl.BlockSpec((D_MODEL, tm), lambda i: (0, i)),
                  pl.BlockSpec((D_MODEL, LANES), lambda i: (0, 0)),
                  pl.BlockSpec((D_MODEL, D_MODEL), lambda i: (0, 0))],
        out_specs=pl.BlockSpec((tm, D_MODEL), lambda i: (i, 0)),
        out_shape=jax.ShapeDtypeStruct((t, D_MODEL), F32),
        compiler_params=_cparams(("parallel",)),
        name="mlstm_out",
    )(x2, h2, ot, gain, w_out)


def _mlstm_layout(w_in):
    nq = N_HEADS * ML_DQK
    pad_heads = lambda w: jnp.pad(w.reshape(D_MODEL, N_HEADS, ML_DQK),
                                  ((0, 0), (0, 0), (0, LANES - ML_DQK))).reshape(D_MODEL, N_HEADS * LANES)
    w_q = pad_heads(w_in[:, :nq])
    w_k = pad_heads(w_in[:, nq:2 * nq])
    rest = w_in[:, 2 * nq:]
    w_rows = jnp.concatenate([w_q, rest], axis=1).T
    return w_k.astype(BF16), w_rows.astype(BF16)


def _da_proj_kernel(x_ref, g_ref, w_ref, pos_ref, inv_ref, wts_ref, gq_ref, gk_ref,
                    cq_ref, ck_ref, q_ref, k_ref, v_ref):
    hn = _rms_rows(x_ref[...], g_ref[...]).astype(BF16)
    ang = pos_ref[...] * inv_ref[...]
    cos = jnp.cos(ang)
    sin = jnp.sin(ang)
    trig_q = cq_ref[0:1, :] * cos + cq_ref[1:2, :] * sin + cq_ref[2:3, :]
    trig_k = ck_ref[0:1, :] * cos + ck_ref[1:2, :] * sin + ck_ref[2:3, :]
    fq = trig_q * gq_ref[...]
    fk = trig_k * gk_ref[...]
    nsub = 2 * N_HEADS
    per_dot = MXU_LANES // LANES
    for u0 in range(0, nsub, per_dot):
        for off, fac, wts, o_ref in ((0, fq, wts_ref[0:1, :], q_ref), (nsub * LANES, fk, wts_ref[1:2, :], k_ref)):
            raw2 = jnp.dot(hn, w_ref[:, off + u0 * LANES: off + (u0 + per_dot) * LANES],
                           preferred_element_type=F32)
            for du in range(per_dot):
                raw = raw2[:, du * LANES:(du + 1) * LANES]
                ms = jnp.sum(raw * raw * wts, axis=-1, keepdims=True) * (1.0 / DA_DHEAD)
                o_ref[:, (u0 + du) * LANES:(u0 + du + 1) * LANES] = (raw * lax.rsqrt(ms + EPS) * fac).astype(BF16)
    v_ref[...] = jnp.dot(hn, w_ref[:, 2 * nsub * LANES:], preferred_element_type=F32).astype(BF16)


def _da_layout(w_qkv, q_g, k_g):
    x1 = list(range(0, 8))
    x2 = list(range(8, 16))
    rest = list(range(16, 64))
    q_groups = [(x1, 1, 0, 0), (x1, 1, 0, 0), (x2, 0, -1, 0), (x2, 0, -1, 0),
                (x2, 1, 0, 0), (x2, 1, 0, 0), (x1, 0, 1, 0), (x1, 0, 1, 0)]
    k_groups = [(x1, 1, 0, 0), (x2, 0, -1, 0), (x1, 1, 0, 0), (x2, 0, -1, 0),
                (x2, 1, 0, 0), (x1, 0, 1, 0), (x2, 1, 0, 0), (x1, 0, 1, 0)]

    def tables(groups):
        src, coef = [], []
        for dims, a, b_, c_ in groups:
            src += dims
            coef += [(a, b_, c_)] * len(dims)
        src += rest
        coef += [(0, 0, 1)] * len(rest)
        pad = LANES - len(src)
        valid = [1.0] * len(src) + [0.0] * pad
        src += [0] * pad
        coef += [(0, 0, 0)] * pad
        return jnp.array(src, jnp.int32), jnp.array(coef, F32).T, jnp.array(valid, F32)

    src_q, coef_q, valid_q = tables(q_groups)
    src_k, coef_k, valid_k = tables(k_groups)
    nsub = 2 * N_HEADS
    wq = w_qkv[:, :nsub * DA_DHEAD].reshape(D_MODEL, nsub, DA_DHEAD)
    wk = w_qkv[:, nsub * DA_DHEAD:2 * nsub * DA_DHEAD].reshape(D_MODEL, nsub, DA_DHEAD)
    wv = w_qkv[:, 2 * nsub * DA_DHEAD:]
    wq_p = (wq[:, :, src_q] * valid_q).reshape(D_MODEL, nsub * LANES)
    wk_p = (wk[:, :, src_k] * valid_k).reshape(D_MODEL, nsub * LANES)
    w_cat = jnp.concatenate([wq_p, wk_p, wv], axis=1).astype(BF16)
    lanes = jnp.arange(LANES)
    plain = (lanes >= 64) & (lanes < 112)
    wts_q = ((lanes < 8) | ((lanes >= 16) & (lanes < 24)) | plain).astype(F32)
    wts_k = ((lanes < 16) | plain).astype(F32)
    wts = jnp.concatenate([wts_q[None], wts_k[None], jnp.zeros((6, LANES), F32)], axis=0)
    gq = (q_g[src_q] * valid_q).reshape(1, LANES) * (DA_DHEAD ** -0.5 * math.log2(math.e))
    gk = (k_g[src_k] * valid_k).reshape(1, LANES)
    freq = jnp.where(lanes < 64, lanes % ROPE_HALF, 0)
    inv = ROPE_THETA ** (-freq.astype(F32) * 2.0 / (2 * ROPE_HALF))
    inv = jnp.where(lanes < 64, inv, 0.0).reshape(1, LANES)
    pad8 = lambda a: jnp.concatenate([a, jnp.zeros((5, LANES), F32)], axis=0)
    return w_cat, inv, wts, gq, gk, pad8(coef_q), pad8(coef_k)


def _da_proj(x2, norm_g, w_qkv, q_g, k_g, pos, tm):
    t = x2.shape[0]
    w_cat, inv, wts, gq, gk, cq, ck = _da_layout(w_qkv, q_g, k_g)
    nq = 2 * N_HEADS * LANES
    const = lambda shp: pl.BlockSpec(shp, lambda i: (0,) * len(shp))
    return pl.pallas_call(
        _da_proj_kernel,
        grid=(t // tm,),
        in_specs=[pl.BlockSpec((tm, D_MODEL), lambda i: (i, 0)),
                  const((1, D_MODEL)), const(w_cat.shape),
                  pl.BlockSpec((tm, 1), lambda i: (i, 0)),
                  const((1, LANES)), const((8, LANES)), const((1, LANES)), const((1, LANES)),
                  const((8, LANES)), const((8, LANES))],
        out_specs=[pl.BlockSpec((tm, nq), lambda i: (i, 0)),
                   pl.BlockSpec((tm, nq), lambda i: (i, 0)),
                   pl.BlockSpec((tm, D_MODEL), lambda i: (i, 0))],
        out_shape=[jax.ShapeDtypeStruct((t, nq), BF16),
                   jax.ShapeDtypeStruct((t, nq), BF16),
                   jax.ShapeDtypeStruct((t, D_MODEL), BF16)],
        compiler_params=_cparams(("parallel",)),
        name="da_proj",
    )(x2, norm_g.reshape(1, D_MODEL), w_cat, pos, inv, wts, gq, gk, cq, ck)


def _da_attn_kernel(q_ref, k_ref, v_ref, lam_ref, sg_ref, o_ref, s_sc, *, lambda_init, kb):
    out_scale = 1.0 - lambda_init
    lam_v = lam_ref[...]
    lam = (jnp.exp(jnp.sum(lam_v[0:1] * lam_v[1:2], axis=-1, keepdims=True))
           - jnp.exp(jnp.sum(lam_v[2:3] * lam_v[3:4], axis=-1, keepdims=True))
           + lambda_init)
    tq = q_ref.shape[0]
    th = tq // 2
    n_blk = k_ref.shape[0] // kb
    subs = (0, 1)

    ones_blk = (lax.broadcasted_iota(jnp.int32, (kb, LANES), 1) == 0).astype(BF16)

    def score_block(qs, r0, blk, m_part):
        rows = slice(blk * kb, (blk + 1) * kb)
        for c in subs:
            s = lax.dot_general(qs[c], k_ref[rows, c * LANES:(c + 1) * LANES], (((1,), (1,)), ((), ())),
                                preferred_element_type=F32)
            s_sc[c, r0:r0 + th, rows] = s
            for j in range(kb // LANES):
                m_part[c] = jnp.maximum(m_part[c], s[:, j * LANES:(j + 1) * LANES])

    def attend_block(m, r0, blk, acc):
        rows = slice(blk * kb, (blk + 1) * kb)
        v_aug = jnp.concatenate([v_ref[rows, :], ones_blk], axis=1)
        for c in subs:
            p = jnp.exp2(s_sc[c, r0:r0 + th, rows] - m[c])
            acc[c] = acc[c] + jnp.dot(p.astype(BF16), v_aug, preferred_element_type=F32)

    def finish(r0, acc):
        outs = [acc[c][:, :LANES] / acc[c][:, LANES:LANES + 1] for c in subs]
        o = outs[0] - lam * outs[1]
        ms = jnp.mean(o * o, axis=-1, keepdims=True)
        o_ref[r0:r0 + th, :] = (o * lax.rsqrt(ms + EPS) * sg_ref[...]) * out_scale

    new_max = lambda: [jnp.full((th, LANES), -jnp.inf, F32) for _ in subs]
    new_acc = lambda: [jnp.zeros((th, 2 * LANES), F32) for _ in subs]
    row_max = lambda m_part: [jnp.max(m_part[c], axis=-1, keepdims=True) for c in subs]

    q_a = [q_ref[:th, c * LANES:(c + 1) * LANES] for c in subs]
    mp_a, acc_a = new_max(), new_acc()
    for blk in range(n_blk):
        score_block(q_a, 0, blk, mp_a)
    m_a = row_max(mp_a)
    bits = lax.bitcast_convert_type(m_a[0] + m_a[1], jnp.uint32)
    zero = ((bits >> 16) >> 16).astype(F32)
    q_b = [(q_ref[th:, c * LANES:(c + 1) * LANES].astype(F32) + zero).astype(BF16) for c in subs]
    mp_b, acc_b = new_max(), new_acc()
    for blk in range(n_blk):
        score_block(q_b, th, blk, mp_b)
        attend_block(m_a, 0, blk, acc_a)
    finish(0, acc_a)
    m_b = row_max(mp_b)
    for blk in range(n_blk):
        attend_block(m_b, th, blk, acc_b)
    finish(th, acc_b)


DA_KEY_BLOCK = 512


def _da_attn(q, k, v, lam4, subln_g, b, s, lambda_init, tq):
    tq = min(tq, s)
    kb = min(DA_KEY_BLOCK, s)
    return pl.pallas_call(
        functools.partial(_da_attn_kernel, lambda_init=lambda_init, kb=kb),
        grid=(b, N_HEADS, s // tq),
        in_specs=[pl.BlockSpec((None, tq, 2 * LANES), lambda bi, h, i: (bi, i, h)),
                  pl.BlockSpec((None, s, 2 * LANES), lambda bi, h, i: (bi, 0, h)),
                  pl.BlockSpec((None, s, LANES), lambda bi, h, i: (bi, 0, h)),
                  pl.BlockSpec((4, DA_DHEAD), lambda bi, h, i: (0, 0)),
                  pl.BlockSpec((1, LANES), lambda bi, h, i: (0, 0))],
        out_specs=pl.BlockSpec((None, tq, LANES), lambda bi, h, i: (bi, i, h)),
        out_shape=jax.ShapeDtypeStruct((b, s, D_MODEL), F32),
        scratch_shapes=[pltpu.VMEM((2, tq, s), F32)],
        compiler_params=_cparams(("parallel", "parallel", "arbitrary")),
        name="da_attn",
    )(q, k, v, lam4, subln_g.reshape(1, LANES))


def _resid_matmul_kernel(x_ref, a_ref, w_ref, out_ref):
    out_ref[...] = x_ref[...] + jnp.dot(a_ref[...].astype(BF16), w_ref[...], preferred_element_type=F32)


def _resid_matmul(x2, a2, w, tm):
    t = x2.shape[0]
    return pl.pallas_call(
        _resid_matmul_kernel,
        grid=(t // tm,),
        in_specs=[pl.BlockSpec((tm, D_MODEL), lambda i: (i, 0)),
                  pl.BlockSpec((tm, D_MODEL), lambda i: (i, 0)),
                  pl.BlockSpec((D_MODEL, D_MODEL), lambda i: (0, 0))],
        out_specs=pl.BlockSpec((tm, D_MODEL), lambda i: (i, 0)),
        out_shape=jax.ShapeDtypeStruct((t, D_MODEL), F32),
        compiler_params=_cparams(("parallel",)),
        name="resid_matmul",
    )(x2, a2, w)


N_EXTRACT = PEER_TOPK + 1
A_ROWS = 24
N_FULL_ROWS = 8


RANK_NONE = 64.0


def _oddeven_merge_sort_pairs(n):
    pairs = []

    def merge(lo, m, r):
        step = 2 * r
        if step < m:
            merge(lo, m, step)
            merge(lo + r, m, step)
            pairs.extend((i, i + r) for i in range(lo + r, lo + m - r, step))
        else:
            pairs.append((lo, lo + r))

    def sort(lo, m):
        if m > 1:
            sort(lo, m // 2)
            sort(lo + m // 2, m // 2)
            merge(lo, m, 1)

    sort(0, n)
    return pairs


def _extract_desc(vals, a_sc):
    n_grp = vals.shape[0] // SUBLANES
    for lt in range(vals.shape[1] // LANES):
        ls = slice(lt * LANES, (lt + 1) * LANES)
        col = [vals[v * SUBLANES:(v + 1) * SUBLANES, ls] for v in range(n_grp)]
        for i, j in _oddeven_merge_sort_pairs(n_grp):
            col[i], col[j] = jnp.maximum(col[i], col[j]), jnp.minimum(col[i], col[j])
        for t in range(N_EXTRACT):
            mx = jnp.max(col[0], axis=0, keepdims=True)
            a_sc[t:t + 1, ls] = mx
            remaining = N_EXTRACT - 1 - t
            if remaining:
                hit = col[0] == mx
                for v in range(min(remaining, n_grp)):
                    below = col[v + 1] if v + 1 < n_grp else -jnp.inf
                    col[v] = jnp.where(hit, below, col[v])


def _ranks_of(vals, a):
    rank = jnp.full(vals.shape, RANK_NONE, F32)
    for qq in range(N_EXTRACT):
        rank = jnp.where(vals == a[qq:qq + 1, :], float(qq), rank)
    return rank


def _dup_bf16_words(x):
    hi = lax.bitcast_convert_type(x.astype(BF16).astype(F32), jnp.uint32)
    return hi | (hi >> 16)


def _peer_route_kernel(x_ref, g_ref, wq_ref, keys_ref, hnt_ref, cnt_ref, e1z_ref, r2_ref, e2_ref,
                       a1_sc, a2_sc, c_sc):
    tm = x_ref.shape[0]
    hn = _rms_rows(x_ref[...], g_ref[...])
    hnt_ref[...] = hn.T.astype(BF16)
    q = jnp.dot(hn.astype(BF16), wq_ref[...], preferred_element_type=F32).astype(BF16)
    neg = jnp.full((A_ROWS, tm), -jnp.inf, F32)
    for h in range(N_HEADS):
        s_t = []
        for p, a_sc in ((0, a1_sc), (1, a2_sc)):
            hp = 2 * h + p
            st = lax.dot_general(keys_ref[hp], q[:, hp * LANES:(hp + 1) * LANES],
                                 (((1,), (1,)), ((), ())), preferred_element_type=F32)
            a_sc[...] = neg
            _extract_desc(st, a_sc)
            s_t.append(st)
        a1 = a1_sc[...]
        a2 = a2_sc[...]
        rank2 = _ranks_of(s_t[1], a2)
        blocks = [a1[0:1, :] + a2]
        blocks += [a1[p:p + 1, :] + a2[0:SUBLANES, :] for p in range(1, N_FULL_ROWS)]
        blocks.append(a1[N_FULL_ROWS:, :] + a2[0:1, :])
        blocks.append(jnp.full((N_KEYS - sum(blk.shape[0] for blk in blocks), tm), -jnp.inf, F32))
        c_sc[...] = neg
        _extract_desc(jnp.concatenate(blocks, axis=0), c_sc)
        cv = c_sc[...]
        topk_rows = lax.broadcasted_iota(jnp.int32, (A_ROWS, tm), 0) < PEER_TOPK
        z = jnp.sum(jnp.where(topk_rows, jnp.exp(cv - cv[0:1, :]), 0.0), axis=0, keepdims=True)
        tau = 0.5 * (cv[PEER_TOPK - 1:PEER_TOPK, :] + cv[PEER_TOPK:PEER_TOPK + 1, :])
        s1, s2 = s_t
        thr_rank = tau - a1
        cnt_rank = jnp.zeros_like(a1)
        for qq in range(N_EXTRACT):
            cnt_rank = cnt_rank + jnp.where(a2[qq:qq + 1, :] >= thr_rank, 1.0, 0.0)
        cnt = jnp.zeros_like(s1)
        for pp in range(N_EXTRACT):
            cnt = jnp.where(s1 == a1[pp:pp + 1, :], cnt_rank[pp:pp + 1, :], cnt)
        cnt_ref[h] = _dup_bf16_words(cnt)
        e1z_ref[h] = _dup_bf16_words(jnp.exp(s1 - a1[0:1, :]) * (0.5 / z))
        r2_ref[h] = pltpu.bitcast(rank2.astype(BF16), jnp.uint32)
        e2_ref[h] = pltpu.bitcast(jnp.exp(s2 - a2[0:1, :]).astype(BF16), jnp.uint32)


def _peer_route(x2, norm_g, wq, keys, tm):
    t = x2.shape[0]
    row_spec = pl.BlockSpec((N_HEADS, N_KEYS, tm), lambda i: (0, 0, i))
    pair_spec = pl.BlockSpec((N_HEADS, N_KEYS // 2, tm), lambda i: (0, 0, i))
    row_tab = jax.ShapeDtypeStruct((N_HEADS, N_KEYS, t), jnp.uint32)
    pair_tab = jax.ShapeDtypeStruct((N_HEADS, N_KEYS // 2, t), jnp.uint32)
    return pl.pallas_call(
        _peer_route_kernel,
        grid=(t // tm,),
        in_specs=[pl.BlockSpec((tm, D_MODEL), lambda i: (i, 0)),
                  pl.BlockSpec((1, D_MODEL), lambda i: (0, 0)),
                  pl.BlockSpec(wq.shape, lambda i: (0, 0)),
                  pl.BlockSpec(keys.shape, lambda i: (0, 0, 0))],
        out_specs=[pl.BlockSpec((D_MODEL, tm), lambda i: (0, i)), row_spec, row_spec, pair_spec, pair_spec],
        out_shape=[jax.ShapeDtypeStruct((D_MODEL, t), BF16), row_tab, row_tab, pair_tab, pair_tab],
        scratch_shapes=[pltpu.VMEM((A_ROWS, tm), F32) for _ in range(3)],
        compiler_params=_cparams(("parallel",)),
        name="peer_route",
    )(x2, norm_g.reshape(1, D_MODEL), wq, keys)


PEER_EC = SUBLANES * N_KEYS
PIPE_STAGES = 3
I_GROUP = 2


def _peer_expert_kernel(hnt_ref, u_ref, vt_ref, cnt_ref, e1z_ref, r2_ref, e2_ref, x_ref, out_ref,
                        acc_sc, s0_sc, s1_sc, p0_sc, p1_sc, *, ec):
    g = pl.program_id(1)
    tm = hnt_ref.shape[1]
    n_i = ec // N_KEYS

    def row_bf16(rows, k, ls):
        words = jnp.broadcast_to(rows[k:k + 1, ls], (N_KEYS // 2, LANES))
        return pltpu.bitcast(words, BF16)

    zero = jnp.zeros((), BF16)

    def gate_tile(ci, lt, s_in, p_out):
        ls = slice(lt * LANES, (lt + 1) * LANES)
        for i0 in range(0, n_i, I_GROUP):
            gates = [jnp.zeros((N_KEYS, LANES), BF16) for _ in range(I_GROUP)]
            for h in range(N_HEADS):
                r2t = pltpu.bitcast(r2_ref[h][:, ls], BF16)
                e2t = pltpu.bitcast(e2_ref[h][:, ls], BF16)
                tile0 = pl.multiple_of(ci * n_i + (i0 // SUBLANES) * SUBLANES, SUBLANES)
                cnt8 = cnt_ref[h, pl.ds(tile0, SUBLANES), :]
                e1z8 = e1z_ref[h, pl.ds(tile0, SUBLANES), :]
                for kk in range(I_GROUP):
                    k = i0 % SUBLANES + kk
                    sel = jnp.where(r2t < row_bf16(cnt8, k, ls), e2t, zero)
                    gates[kk] = gates[kk] + sel * row_bf16(e1z8, k, ls)
            for kk in range(I_GROUP):
                r0 = (i0 + kk) * N_KEYS
                s = s_in[r0:r0 + N_KEYS, ls].astype(BF16)
                act2 = s * (1.0 + lax.erf(s * (2.0 ** -0.5)))
                p_out[r0 // 2:(r0 + N_KEYS) // 2, ls] = pltpu.bitcast(gates[kk] * act2, jnp.uint32)

    def pipeline_step(c, u_blk, vt_blk, s_in, s_out, p_in, p_out, run_out, run_gate, run_pre):
        tiles_per_slab = MXU_LANES // LANES
        for sb in range(tm // MXU_LANES):
            cs = slice(sb * MXU_LANES, (sb + 1) * MXU_LANES)
            if run_out:
                acc_sc[:, cs] += jnp.dot(vt_blk, pltpu.bitcast(p_in[:, cs], BF16),
                                         preferred_element_type=F32)
            if run_gate:
                gate_tile(c - 1, sb * tiles_per_slab, s_in, p_out)
            if run_pre:
                s_out[:, cs] = jnp.dot(u_blk, hnt_ref[:, cs], preferred_element_type=F32)
            if run_gate:
                for lt in range(sb * tiles_per_slab + 1, (sb + 1) * tiles_per_slab):
                    gate_tile(c - 1, lt, s_in, p_out)

    def grid_step(first, last):
        even = (u_ref[:ec, :], vt_ref[:, :ec], s1_sc, s0_sc, p1_sc, p0_sc)
        odd = (u_ref[ec:, :], vt_ref[:, ec:], s0_sc, s1_sc, p0_sc, p1_sc)
        pipeline_step(2 * g, *even, run_out=not first, run_gate=not first, run_pre=not last)
        pipeline_step(2 * g + 1, *odd, run_out=not first, run_gate=not last, run_pre=not last)

    last_g = pl.num_programs(1) - 1

    @pl.when(g == 0)
    def _():
        acc_sc[...] = jnp.zeros_like(acc_sc)
        grid_step(True, False)

    @pl.when((g > 0) & (g < last_g))
    def _():
        grid_step(False, False)

    @pl.when(g == last_g)
    def _():
        grid_step(False, True)
        out_ref[...] = x_ref[...] + acc_sc[...].T


def _peer_expert(x2, hnt, u, vt, cnt, e1z, r2, e2, tm, ec):
    t = x2.shape[0]
    assert ec % (SUBLANES * N_KEYS) == 0 and (N_EXPERTS // ec) % 2 == 0
    n_chunks = N_EXPERTS // ec
    n_steps = (n_chunks + PIPE_STAGES - 1) // 2
    last_u = n_chunks // 2 - 1
    row_spec = pl.BlockSpec((N_HEADS, N_KEYS, tm), lambda i, g: (0, 0, i))
    pair_spec = pl.BlockSpec((N_HEADS, N_KEYS // 2, tm), lambda i, g: (0, 0, i))
    return pl.pallas_call(
        functools.partial(_peer_expert_kernel, ec=ec),
        grid=(t // tm, n_steps),
        in_specs=[pl.BlockSpec((D_MODEL, tm), lambda i, g: (0, i)),
                  pl.BlockSpec((2 * ec, D_MODEL), lambda i, g: (jnp.minimum(g, last_u), 0)),
                  pl.BlockSpec((D_MODEL, 2 * ec), lambda i, g: (0, jnp.maximum(g - 1, 0))),
                  row_spec, row_spec, pair_spec, pair_spec,
                  pl.BlockSpec((tm, D_MODEL), lambda i, g: (i, 0))],
        out_specs=pl.BlockSpec((tm, D_MODEL), lambda i, g: (i, 0)),
        out_shape=jax.ShapeDtypeStruct((t, D_MODEL), F32),
        scratch_shapes=[pltpu.VMEM((D_MODEL, tm), F32),
                        pltpu.VMEM((ec, tm), F32), pltpu.VMEM((ec, tm), F32),
                        pltpu.VMEM((ec // 2, tm), jnp.uint32), pltpu.VMEM((ec // 2, tm), jnp.uint32)],
        compiler_params=_cparams(("parallel", "arbitrary")),
        name="peer_expert",
    )(hnt, u, vt, cnt, e1z, r2, e2, x2)


def _peer_ffn(x2, norm_g, w_query, sub_keys, u_table, v_table, tm_route, tm_exp, ec):
    keys = sub_keys.reshape(2 * N_HEADS, N_KEYS, LANES).astype(BF16)
    hnt, cnt, e1z, r2, e2 = _peer_route(x2, norm_g, w_query.astype(BF16), keys, tm_route)
    return _peer_expert(x2, hnt, u_table.astype(BF16), v_table.T.astype(BF16), cnt, e1z, r2, e2, tm_exp, ec)


def kernel(x, positions, l0_norm_mix, l0_w_in, l0_b_gates, l0_head_norm, l0_w_out, l0_norm_ffn, l0_peer_wq, l0_peer_keys, l0_peer_u, l0_peer_v, l1_norm_mix, l1_w_qkv, l1_q_norm, l1_k_norm, l1_lambda_q1, l1_lambda_k1, l1_lambda_q2, l1_lambda_k2, l1_subln, l1_w_out, l1_norm_ffn, l1_peer_wq, l1_peer_keys, l1_peer_u, l1_peer_v):
    b, s, d = x.shape
    t = b * s
    tm = min(512, t)
    x2 = x.reshape(t, d)

    w_k, w_rows = _mlstm_layout(l0_w_in)
    k, qt, vt, ot, gt = _mlstm_proj(x2, l0_norm_mix, w_k, w_rows, tm)
    h2 = _mlstm(k, qt, vt, gt, l0_b_gates, b, s)
    x2 = _mlstm_out(x2, h2, ot, l0_head_norm, l0_w_out.astype(BF16), tm)
    x2 = _peer_ffn(x2, l0_norm_ffn, l0_peer_wq, l0_peer_keys, l0_peer_u, l0_peer_v, min(256, t), tm, PEER_EC)

    lambda_init = 0.8 - 0.6 * math.exp(-0.3 * 1)
    pos = positions.reshape(t, 1).astype(F32)
    qa, ka, va = _da_proj(x2, l1_norm_mix, l1_w_qkv, l1_q_norm, l1_k_norm, pos, tm)
    to3 = lambda a: a.reshape(b, s, a.shape[-1])
    lam4 = jnp.stack([l1_lambda_q1, l1_lambda_k1, l1_lambda_q2, l1_lambda_k2]).astype(F32)
    att = _da_attn(to3(qa), to3(ka), to3(va), lam4, l1_subln, b, s, lambda_init, 512)
    x2 = _resid_matmul(x2, att.reshape(t, D_MODEL), l1_w_out.astype(BF16), tm)
    x2 = _peer_ffn(x2, l1_norm_ffn, l1_peer_wq, l1_peer_keys, l1_peer_u, l1_peer_v, min(256, t), tm, PEER_EC)
    return x2.reshape(b, s, d)
```

```python
import functools
import math

import jax
import jax.numpy as jnp
from jax import lax
from jax.experimental import pallas as pl
from jax.experimental.pallas import tpu as pltpu

F32 = jnp.float32
BF16 = jnp.bfloat16
HIGHEST = lax.Precision.HIGHEST

D_MODEL = 1024
EPS = 1e-6
LANES = 128
SUBLANES = 8
MXU_LANES = 256
N_HEADS = 8
ML_DQK = 64
ML_CHUNK = 256
ML_M_INIT = -1e30
DA_DHEAD = 64
ROPE_THETA = 500000.0
ROPE_HALF = 8
N_KEYS = 128
N_EXPERTS = N_KEYS * N_KEYS
PEER_TOPK = 16
VMEM_LIMIT_BYTES = 56 * 1024 * 1024


def _cparams(sem):
    return pltpu.CompilerParams(dimension_semantics=sem, vmem_limit_bytes=VMEM_LIMIT_BYTES)


def _rms_rows(x, g):
    ms = jnp.mean(x * x, axis=-1, keepdims=True)
    return x * lax.rsqrt(ms + EPS) * g


N_GATES = 4 * N_HEADS


def _mlstm_proj_kernel(x_ref, g_ref, wk_ref, wt_ref, k_ref, qt_ref, vt_ref, ot_ref, gt_ref):
    hn = _rms_rows(x_ref[...], g_ref[...]).astype(BF16)
    k_ref[...] = jnp.dot(hn, wk_ref[...], preferred_element_type=F32).astype(BF16)

    def feature_major(r0, n):
        return lax.dot_general(wt_ref[r0:r0 + n, :], hn, (((1,), (1,)), ((), ())), preferred_element_type=F32)

    qt_ref[...] = (feature_major(0, D_MODEL) * (ML_DQK ** -0.5)).astype(BF16)
    vt_ref[...] = feature_major(D_MODEL, D_MODEL).astype(BF16)
    ot_ref[...] = feature_major(2 * D_MODEL, D_MODEL)
    gt_ref[...] = feature_major(3 * D_MODEL, N_GATES)


def _mlstm_proj(x2, g, w_k, w_rows, tm):
    t = x2.shape[0]
    fm = lambda n, dt: (pl.BlockSpec((n, tm), lambda i: (0, i)), jax.ShapeDtypeStruct((n, t), dt))
    outs = [(pl.BlockSpec((tm, D_MODEL), lambda i: (i, 0)), jax.ShapeDtypeStruct((t, D_MODEL), BF16)),
            fm(D_MODEL, BF16), fm(D_MODEL, BF16), fm(D_MODEL, F32), fm(N_GATES, F32)]
    return pl.pallas_call(
        _mlstm_proj_kernel,
        grid=(t // tm,),
        in_specs=[pl.BlockSpec((tm, D_MODEL), lambda i: (i, 0)),
                  pl.BlockSpec((1, D_MODEL), lambda i: (0, 0)),
                  pl.BlockSpec(w_k.shape, lambda i: (0, 0)),
                  pl.BlockSpec(w_rows.shape, lambda i: (0, 0))],
        out_specs=[o[0] for o in outs],
        out_shape=[o[1] for o in outs],
        compiler_params=_cparams(("parallel",)),
        name="mlstm_proj",
    )(x2, g.reshape(1, D_MODEL), w_k, w_rows)


def _mlstm_kernel(kf_ref, qtf_ref, vtf_ref, gtf_ref, kb_ref, qtb_ref, vtb_ref, gtb_ref, bias_ref, mt_ref,
                  of_ref, ob_ref, ct_sc, m_sc, *, L):
    @pl.when(pl.program_id(1) == 0)
    def _():
        ct_sc[...] = jnp.zeros_like(ct_sc)
        m_sc[...] = jnp.full_like(m_sc, ML_M_INIT)

    _mlstm_chunk(0, kf_ref, qtf_ref, vtf_ref, gtf_ref, bias_ref, mt_ref, of_ref, ct_sc, m_sc, L)
    _mlstm_chunk(1, kb_ref, qtb_ref, vtb_ref, gtb_ref, bias_ref, mt_ref, ob_ref, ct_sc, m_sc, L)


def _mlstm_chunk(d, k_ref, qt_ref, vt_ref, gt_ref, bias_ref, mt_ref, o_ref, ct_sc, m_sc, L):
    row0 = d * 2 * N_HEADS
    g = gt_ref[row0:row0 + 2 * N_HEADS, :] + bias_ref[row0:row0 + 2 * N_HEADS, :]
    li = g[:N_HEADS]
    gf = g[N_HEADS:]
    lf = jnp.minimum(gf, 0.0) - jnp.log1p(jnp.exp(-jnp.abs(gf)))
    mask_f = mt_ref[d]
    b = jnp.dot(lf, mask_f, precision=HIGHEST, preferred_element_type=F32)
    b_tot = jnp.min(b, axis=1, keepdims=True)
    r = li - b
    a = b_tot + r
    m_loc = jnp.max(a, axis=1, keepdims=True)
    w = jnp.exp(a - m_loc)
    m_prev = m_sc[d, :, 0:1]
    g_in = b + m_prev
    allowed = mask_f > 0.5
    r_cols = jnp.concatenate([r, jnp.zeros((LANES - N_HEADS, L), F32)], axis=0).T

    def masked_r(h):
        return jnp.where(allowed, r_cols[:, h:h + 1], -jnp.inf)

    c_max = jnp.concatenate([jnp.max(masked_r(h), axis=0, keepdims=True) for h in range(N_HEADS)], axis=0)
    m_t = jnp.maximum(g_in, b + c_max)
    u = b - m_t
    inter = jnp.exp(g_in - m_t)
    e_neg = jnp.exp(-m_t)
    m_new = jnp.maximum(b_tot + m_prev, m_loc)
    s_prev = jnp.exp(b_tot + m_prev - m_new)
    s_loc = jnp.exp(m_loc - m_new)
    ones_rows = (lax.broadcasted_iota(jnp.int32, (LANES, L), 0) == 0).astype(BF16)

    for h in range(N_HEADS):
        rows = slice(h * LANES, (h + 1) * LANES)
        hh = slice(h, h + 1)
        qt = qt_ref[rows, :]
        k = k_ref[:, rows]
        v_aug = jnp.concatenate([vt_ref[rows, :], ones_rows], axis=0)
        ct_prev = ct_sc[d, h]
        e_t = jnp.exp(masked_r(h) + u[hh, :])
        p_t = (jnp.dot(k, qt, preferred_element_type=F32) * e_t).astype(BF16)
        nd = (jnp.dot(v_aug, p_t, preferred_element_type=F32)
              + inter[hh, :] * jnp.dot(ct_prev.astype(BF16), qt, preferred_element_type=F32))
        den = jnp.maximum(jnp.abs(nd[LANES:LANES + 1, :]), e_neg[hh, :])
        o_ref[rows, :] = nd[:LANES, :] / den
        wv = (v_aug.astype(F32) * w[hh, :]).astype(BF16)
        ct_loc = jnp.dot(wv, k, preferred_element_type=F32)
        ct_sc[d, h] = s_prev[hh, :] * ct_prev + s_loc[hh, :] * ct_loc
    m_sc[d] = jnp.broadcast_to(m_new, (N_HEADS, LANES))


def _mlstm(k, qt, vt, gt, bias, b, s):
    L = min(ML_CHUNK, s)
    nc = s // L
    tri = jnp.tril(jnp.ones((L, L), F32))
    masks_t = jnp.stack([tri.T, tri])
    bias_b = jnp.broadcast_to(bias.reshape(N_GATES, 1).astype(F32), (N_GATES, L))
    fwd = lambda bi, c: bi * nc + c
    bwd = lambda bi, c: bi * nc + nc - 1 - c

    def seq_specs(pos):
        fm = lambda n: pl.BlockSpec((n, L), lambda bi, c: (0, pos(bi, c)))
        return [pl.BlockSpec((L, D_MODEL), lambda bi, c: (pos(bi, c), 0)), fm(D_MODEL), fm(D_MODEL), fm(N_GATES)]

    out_spec = lambda pos: pl.BlockSpec((D_MODEL, L), lambda bi, c: (0, pos(bi, c)))
    out = jax.ShapeDtypeStruct((D_MODEL, b * s), F32)
    return pl.pallas_call(
        functools.partial(_mlstm_kernel, L=L),
        grid=(b, nc),
        in_specs=seq_specs(fwd) + seq_specs(bwd) + [pl.BlockSpec((N_GATES, L), lambda bi, c: (0, 0)),
                                                    pl.BlockSpec((2, L, L), lambda bi, c: (0, 0, 0))],
        out_specs=[out_spec(fwd), out_spec(bwd)],
        out_shape=[out, out],
        scratch_shapes=[pltpu.VMEM((2, N_HEADS, 2 * LANES, LANES), F32),
                        pltpu.VMEM((2, N_HEADS, LANES), F32)],
        compiler_params=_cparams(("parallel", "arbitrary")),
        name="mlstm_scan",
    )(k, qt, vt, gt, k, qt, vt, gt, bias_b, masks_t)


def _mlstm_out_kernel(x_ref, hf_ref, hb_ref, ot_ref, hg_ref, w_ref, out_ref):
    tm = x_ref.shape[0]
    hs = hf_ref[...] + hb_ref[...]
    parts = []
    for h in range(N_HEADS):
        blk = hs[h * LANES:(h + 1) * LANES, :]
        ms = jnp.mean(blk * blk, axis=0, keepdims=True)
        parts.append(blk * lax.rsqrt(ms + EPS))
    gain = jnp.concatenate([hg_ref[...]] * (tm // LANES), axis=1)
    hn_t = (jnp.concatenate(parts, axis=0) * gain * jax.nn.sigmoid(ot_ref[...])).astype(BF16)
    out_ref[...] = x_ref[...] + lax.dot_general(hn_t, w_ref[...], (((0,), (0,)), ((), ())),
                                                preferred_element_type=F32)


def _mlstm_out(x2, hf, hb, ot, head_g, w_out, tm):
    t = x2.shape[0]
    gain = jnp.broadcast_to(head_g.reshape(D_MODEL, 1).astype(F32), (D_MODEL, LANES))
    return pl.pallas_call(
        _mlstm_out_kernel,
        grid=(t // tm,),
        in_specs=[pl.BlockSpec((tm, D_MODEL), lambda i: (i, 0)),
                  pl.BlockSpec((D_MODEL, tm), lambda i: (0, i)),
                  pl.BlockSpec((D_MODEL, tm), lambda i: (0, i)),
                  pl.BlockSpec((D_MODEL, tm), lambda i: (0, i)),
                  pl.BlockSpec((D_MODEL, LANES), lambda i: (0, 0)),
                  pl.BlockSpec((D_MODEL, D_MODEL), lambda i: (0, 0))],
        out_specs=pl.BlockSpec((tm, D_MODEL), lambda i: (i, 0)),
        out_shape=jax.ShapeDtypeStruct((t, D_MODEL), F32),
        compiler_params=_cparams(("parallel",)),
        name="mlstm_out",
    )(x2, hf, hb, ot, gain, w_out)


def _mlstm_layout(w_in):
    nq = N_HEADS * ML_DQK
    pad_heads = lambda w: jnp.pad(w.reshape(D_MODEL, N_HEADS, ML_DQK),
                                  ((0, 0), (0, 0), (0, LANES - ML_DQK))).reshape(D_MODEL, N_HEADS * LANES)
    w_q = pad_heads(w_in[:, :nq])
    w_k = pad_heads(w_in[:, nq:2 * nq])
    rest = w_in[:, 2 * nq:]
    w_rows = jnp.concatenate([w_q, rest], axis=1).T
    return w_k.astype(BF16), w_rows.astype(BF16)


def _da_proj_kernel(x_ref, g_ref, w_ref, pos_ref, inv_ref, wts_ref, gq_ref, gk_ref,
                    cq_ref, ck_ref, q_ref, k_ref, v_ref):
    hn = _rms_rows(x_ref[...], g_ref[...]).astype(BF16)
    ang = pos_ref[...] * inv_ref[...]
    cos = jnp.cos(ang)
    sin = jnp.sin(ang)
    trig_q = cq_ref[0:1, :] * cos + cq_ref[1:2, :] * sin + cq_ref[2:3, :]
    trig_k = ck_ref[0:1, :] * cos + ck_ref[1:2, :] * sin + ck_ref[2:3, :]
    fq = trig_q * gq_ref[...]
    fk = trig_k * gk_ref[...]
    nsub = 2 * N_HEADS
    per_dot = MXU_LANES // LANES
    for u0 in range(0, nsub, per_dot):
        for off, fac, wts, o_ref in ((0, fq, wts_ref[0:1, :], q_ref), (nsub * LANES, fk, wts_ref[1:2, :], k_ref)):
            raw2 = jnp.dot(hn, w_ref[:, off + u0 * LANES: off + (u0 + per_dot) * LANES],
                           preferred_element_type=F32)
            for du in range(per_dot):
                raw = raw2[:, du * LANES:(du + 1) * LANES]
                ms = jnp.sum(raw * raw * wts, axis=-1, keepdims=True) * (1.0 / DA_DHEAD)
                o_ref[:, (u0 + du) * LANES:(u0 + du + 1) * LANES] = (raw * lax.rsqrt(ms + EPS) * fac).astype(BF16)
    v_ref[...] = jnp.dot(hn, w_ref[:, 2 * nsub * LANES:], preferred_element_type=F32).astype(BF16)


def _da_layout(w_qkv, q_g, k_g):
    x1 = list(range(0, 8))
    x2 = list(range(8, 16))
    rest = list(range(16, 64))
    q_groups = [(x1, 1, 0, 0), (x1, 1, 0, 0), (x2, 0, -1, 0), (x2, 0, -1, 0),
                (x2, 1, 0, 0), (x2, 1, 0, 0), (x1, 0, 1, 0), (x1, 0, 1, 0)]
    k_groups = [(x1, 1, 0, 0), (x2, 0, -1, 0), (x1, 1, 0, 0), (x2, 0, -1, 0),
                (x2, 1, 0, 0), (x1, 0, 1, 0), (x2, 1, 0, 0), (x1, 0, 1, 0)]

    def tables(groups):
        src, coef = [], []
        for dims, a, b_, c_ in groups:
            src += dims
            coef += [(a, b_, c_)] * len(dims)
        src += rest
        coef += [(0, 0, 1)] * len(rest)
        pad = LANES - len(src)
        valid = [1.0] * len(src) + [0.0] * pad
        src += [0] * pad
        coef += [(0, 0, 0)] * pad
        return jnp.array(src, jnp.int32), jnp.array(coef, F32).T, jnp.array(valid, F32)

    src_q, coef_q, valid_q = tables(q_groups)
    src_k, coef_k, valid_k = tables(k_groups)
    nsub = 2 * N_HEADS
    wq = w_qkv[:, :nsub * DA_DHEAD].reshape(D_MODEL, nsub, DA_DHEAD)
    wk = w_qkv[:, nsub * DA_DHEAD:2 * nsub * DA_DHEAD].reshape(D_MODEL, nsub, DA_DHEAD)
    wv = w_qkv[:, 2 * nsub * DA_DHEAD:]
    wq_p = (wq[:, :, src_q] * valid_q).reshape(D_MODEL, nsub * LANES)
    wk_p = (wk[:, :, src_k] * valid_k).reshape(D_MODEL, nsub * LANES)
    w_cat = jnp.concatenate([wq_p, wk_p, wv], axis=1).astype(BF16)
    lanes = jnp.arange(LANES)
    plain = (lanes >= 64) & (lanes < 112)
    wts_q = ((lanes < 8) | ((lanes >= 16) & (lanes < 24)) | plain).astype(F32)
    wts_k = ((lanes < 16) | plain).astype(F32)
    wts = jnp.concatenate([wts_q[None], wts_k[None], jnp.zeros((6, LANES), F32)], axis=0)
    gq = (q_g[src_q] * valid_q).reshape(1, LANES) * (DA_DHEAD ** -0.5 * math.log2(math.e))
    gk = (k_g[src_k] * valid_k).reshape(1, LANES)
    freq = jnp.where(lanes < 64, lanes % ROPE_HALF, 0)
    inv = ROPE_THETA ** (-freq.astype(F32) * 2.0 / (2 * ROPE_HALF))
    inv = jnp.where(lanes < 64, inv, 0.0).reshape(1, LANES)
    pad8 = lambda a: jnp.concatenate([a, jnp.zeros((5, LANES), F32)], axis=0)
    return w_cat, inv, wts, gq, gk, pad8(coef_q), pad8(coef_k)


def _da_proj(x2, norm_g, w_qkv, q_g, k_g, pos, tm):
    t = x2.shape[0]
    w_cat, inv, wts, gq, gk, cq, ck = _da_layout(w_qkv, q_g, k_g)
    nq = 2 * N_HEADS * LANES
    const = lambda shp: pl.BlockSpec(shp, lambda i: (0,) * len(shp))
    return pl.pallas_call(
        _da_proj_kernel,
        grid=(t // tm,),
        in_specs=[pl.BlockSpec((tm, D_MODEL), lambda i: (i, 0)),
                  const((1, D_MODEL)), const(w_cat.shape),
                  pl.BlockSpec((tm, 1), lambda i: (i, 0)),
                  const((1, LANES)), const((8, LANES)), const((1, LANES)), const((1, LANES)),
                  const((8, LANES)), const((8, LANES))],
        out_specs=[pl.BlockSpec((tm, nq), lambda i: (i, 0)),
                   pl.BlockSpec((tm, nq), lambda i: (i, 0)),
                   pl.BlockSpec((tm, D_MODEL), lambda i: (i, 0))],
        out_shape=[jax.ShapeDtypeStruct((t, nq), BF16),
                   jax.ShapeDtypeStruct((t, nq), BF16),
                   jax.ShapeDtypeStruct((t, D_MODEL), BF16)],
        compiler_params=_cparams(("parallel",)),
        name="da_proj",
    )(x2, norm_g.reshape(1, D_MODEL), w_cat, pos, inv, wts, gq, gk, cq, ck)


def _da_attn_kernel(q_ref, k_ref, v_ref, lam_ref, sg_ref, o_ref, s_sc, *, lambda_init, kb):
    out_scale = 1.0 - lambda_init
    lam_v = lam_ref[...]
    lam = (jnp.exp(jnp.sum(lam_v[0:1] * lam_v[1:2], axis=-1, keepdims=True))
           - jnp.exp(jnp.sum(lam_v[2:3] * lam_v[3:4], axis=-1, keepdims=True))
           + lambda_init)
    tq = q_ref.shape[0]
    th = tq // 2
    n_blk = k_ref.shape[0] // kb
    subs = (0, 1)

    ones_blk = (lax.broadcasted_iota(jnp.int32, (kb, LANES), 1) == 0).astype(BF16)

    def score_block(qs, r0, blk, m_part):
        rows = slice(blk * kb, (blk + 1) * kb)
        for c in subs:
            s = lax.dot_general(qs[c], k_ref[rows, c * LANES:(c + 1) * LANES], (((1,), (1,)), ((), ())),
                                preferred_element_type=F32)
            s_sc[c, r0:r0 + th, rows] = s
            for j in range(kb // LANES):
                m_part[c] = jnp.maximum(m_part[c], s[:, j * LANES:(j + 1) * LANES])

    def attend_block(m, r0, blk, acc):
        rows = slice(blk * kb, (blk + 1) * kb)
        v_aug = jnp.concatenate([v_ref[rows, :], ones_blk], axis=1)
        for c in subs:
            p = jnp.exp2(s_sc[c, r0:r0 + th, rows] - m[c])
            acc[c] = acc[c] + jnp.dot(p.astype(BF16), v_aug, preferred_element_type=F32)

    def finish(r0, acc):
        outs = [acc[c][:, :LANES] / acc[c][:, LANES:LANES + 1] for c in subs]
        o = outs[0] - lam * outs[1]
        ms = jnp.mean(o * o, axis=-1, keepdims=True)
        o_ref[r0:r0 + th, :] = ((o * lax.rsqrt(ms + EPS) * sg_ref[...]) * out_scale).astype(o_ref.dtype)

    new_max = lambda: [jnp.full((th, LANES), -jnp.inf, F32) for _ in subs]
    new_acc = lambda: [jnp.zeros((th, 2 * LANES), F32) for _ in subs]
    row_max = lambda m_part: [jnp.max(m_part[c], axis=-1, keepdims=True) for c in subs]

    q_a = [q_ref[:th, c * LANES:(c + 1) * LANES] for c in subs]
    mp_a, acc_a = new_max(), new_acc()
    for blk in range(n_blk):
        score_block(q_a, 0, blk, mp_a)
    m_a = row_max(mp_a)
    bits = lax.bitcast_convert_type(m_a[0] + m_a[1], jnp.uint32)
    zero = ((bits >> 16) >> 16).astype(F32)
    q_b = [(q_ref[th:, c * LANES:(c + 1) * LANES].astype(F32) + zero).astype(BF16) for c in subs]
    mp_b, acc_b = new_max(), new_acc()
    for blk in range(n_blk):
        score_block(q_b, th, blk, mp_b)
        attend_block(m_a, 0, blk, acc_a)
    finish(0, acc_a)
    m_b = row_max(mp_b)
    for blk in range(n_blk):
        attend_block(m_b, th, blk, acc_b)
    finish(th, acc_b)


DA_KEY_BLOCK = 1024


def _da_attn(q, k, v, lam4, subln_g, b, s, lambda_init, tq):
    tq = min(tq, s)
    kb = min(DA_KEY_BLOCK, s)
    return pl.pallas_call(
        functools.partial(_da_attn_kernel, lambda_init=lambda_init, kb=kb),
        grid=(b, N_HEADS, s // tq),
        in_specs=[pl.BlockSpec((None, tq, 2 * LANES), lambda bi, h, i: (bi, i, h)),
                  pl.BlockSpec((None, s, 2 * LANES), lambda bi, h, i: (bi, 0, h)),
                  pl.BlockSpec((None, s, LANES), lambda bi, h, i: (bi, 0, h)),
                  pl.BlockSpec((4, DA_DHEAD), lambda bi, h, i: (0, 0)),
                  pl.BlockSpec((1, LANES), lambda bi, h, i: (0, 0))],
        out_specs=pl.BlockSpec((None, tq, LANES), lambda bi, h, i: (bi, i, h)),
        out_shape=jax.ShapeDtypeStruct((b, s, D_MODEL), BF16),
        scratch_shapes=[pltpu.VMEM((2, tq, s), F32)],
        compiler_params=_cparams(("parallel", "parallel", "arbitrary")),
        name="da_attn",
    )(q, k, v, lam4, subln_g.reshape(1, LANES))


def _resid_matmul_kernel(x_ref, a_ref, w_ref, out_ref):
    out_ref[...] = x_ref[...] + jnp.dot(a_ref[...], w_ref[...], preferred_element_type=F32)


def _resid_matmul(x2, a2, w, tm):
    t = x2.shape[0]
    return pl.pallas_call(
        _resid_matmul_kernel,
        grid=(t // tm,),
        in_specs=[pl.BlockSpec((tm, D_MODEL), lambda i: (i, 0)),
                  pl.BlockSpec((tm, D_MODEL), lambda i: (i, 0)),
                  pl.BlockSpec((D_MODEL, D_MODEL), lambda i: (0, 0))],
        out_specs=pl.BlockSpec((tm, D_MODEL), lambda i: (i, 0)),
        out_shape=jax.ShapeDtypeStruct((t, D_MODEL), F32),
        compiler_params=_cparams(("parallel",)),
        name="resid_matmul",
    )(x2, a2, w)


N_EXTRACT = PEER_TOPK + 1
A_ROWS = 24
N_FULL_ROWS = 8


RANK_NONE = 64.0


def _oddeven_merge_sort_pairs(n):
    pairs = []

    def merge(lo, m, r):
        step = 2 * r
        if step < m:
            merge(lo, m, step)
            merge(lo + r, m, step)
            pairs.extend((i, i + r) for i in range(lo + r, lo + m - r, step))
        else:
            pairs.append((lo, lo + r))

    def sort(lo, m):
        if m > 1:
            sort(lo, m // 2)
            sort(lo + m // 2, m // 2)
            merge(lo, m, 1)

    sort(0, n)
    return pairs


def _extract_desc(vals, a_sc):
    n_grp = vals.shape[0] // SUBLANES
    for lt in range(vals.shape[1] // LANES):
        ls = slice(lt * LANES, (lt + 1) * LANES)
        col = [vals[v * SUBLANES:(v + 1) * SUBLANES, ls] for v in range(n_grp)]
        for i, j in _oddeven_merge_sort_pairs(n_grp):
            col[i], col[j] = jnp.maximum(col[i], col[j]), jnp.minimum(col[i], col[j])
        for t in range(N_EXTRACT):
            mx = jnp.max(col[0], axis=0, keepdims=True)
            a_sc[t:t + 1, ls] = mx
            remaining = N_EXTRACT - 1 - t
            if remaining:
                hit = col[0] == mx
                for v in range(min(remaining, n_grp)):
                    below = col[v + 1] if v + 1 < n_grp else -jnp.inf
                    col[v] = jnp.where(hit, below, col[v])


def _ranks_of(vals, a):
    rank = jnp.full(vals.shape, RANK_NONE, F32)
    for qq in range(N_EXTRACT):
        rank = jnp.where(vals == a[qq:qq + 1, :], float(qq), rank)
    return rank


def _dup_bf16_words(x):
    hi = lax.bitcast_convert_type(x.astype(BF16).astype(F32), jnp.uint32)
    return hi | (hi >> 16)


def _peer_route_kernel(x_ref, g_ref, wq_ref, keys_ref, hnt_ref, cnt_ref, e1z_ref, r2_ref, e2_ref,
                       a1_sc, a2_sc, c_sc):
    tm = x_ref.shape[0]
    hn = _rms_rows(x_ref[...], g_ref[...])
    hnt_ref[...] = hn.T.astype(BF16)
    q = jnp.dot(hn.astype(BF16), wq_ref[...], preferred_element_type=F32).astype(BF16)
    neg = jnp.full((A_ROWS, tm), -jnp.inf, F32)
    for h in range(N_HEADS):
        s_t = []
        for p, a_sc in ((0, a1_sc), (1, a2_sc)):
            hp = 2 * h + p
            st = lax.dot_general(keys_ref[hp], q[:, hp * LANES:(hp + 1) * LANES],
                                 (((1,), (1,)), ((), ())), preferred_element_type=F32)
            a_sc[...] = neg
            _extract_desc(st, a_sc)
            s_t.append(st)
        a1 = a1_sc[...]
        a2 = a2_sc[...]
        rank2 = _ranks_of(s_t[1], a2)
        blocks = [a1[0:1, :] + a2]
        blocks += [a1[p:p + 1, :] + a2[0:SUBLANES, :] for p in range(1, N_FULL_ROWS)]
        blocks.append(a1[N_FULL_ROWS:, :] + a2[0:1, :])
        blocks.append(jnp.full((N_KEYS - sum(blk.shape[0] for blk in blocks), tm), -jnp.inf, F32))
        c_sc[...] = neg
        _extract_desc(jnp.concatenate(blocks, axis=0), c_sc)
        cv = c_sc[...]
        topk_rows = lax.broadcasted_iota(jnp.int32, (A_ROWS, tm), 0) < PEER_TOPK
        z = jnp.sum(jnp.where(topk_rows, jnp.exp(cv - cv[0:1, :]), 0.0), axis=0, keepdims=True)
        tau = 0.5 * (cv[PEER_TOPK - 1:PEER_TOPK, :] + cv[PEER_TOPK:PEER_TOPK + 1, :])
        s1, s2 = s_t
        thr_rank = tau - a1
        cnt_rank = jnp.zeros_like(a1)
        for qq in range(N_EXTRACT):
            cnt_rank = cnt_rank + jnp.where(a2[qq:qq + 1, :] >= thr_rank, 1.0, 0.0)
        cnt = jnp.zeros_like(s1)
        for pp in range(N_EXTRACT):
            cnt = jnp.where(s1 == a1[pp:pp + 1, :], cnt_rank[pp:pp + 1, :], cnt)
        cnt_ref[h] = _dup_bf16_words(cnt)
        e1z_ref[h] = _dup_bf16_words(jnp.exp(s1 - a1[0:1, :]) * (0.5 / z))
        r2_ref[h] = pltpu.bitcast(rank2.astype(BF16), jnp.uint32)
        e2_ref[h] = pltpu.bitcast(jnp.exp(s2 - a2[0:1, :]).astype(BF16), jnp.uint32)


def _peer_route(x2, norm_g, wq, keys, tm):
    t = x2.shape[0]
    row_spec = pl.BlockSpec((N_HEADS, N_KEYS, tm), lambda i: (0, 0, i))
    pair_spec = pl.BlockSpec((N_HEADS, N_KEYS // 2, tm), lambda i: (0, 0, i))
    row_tab = jax.ShapeDtypeStruct((N_HEADS, N_KEYS, t), jnp.uint32)
    pair_tab = jax.ShapeDtypeStruct((N_HEADS, N_KEYS // 2, t), jnp.uint32)
    return pl.pallas_call(
        _peer_route_kernel,
        grid=(t // tm,),
        in_specs=[pl.BlockSpec((tm, D_MODEL), lambda i: (i, 0)),
                  pl.BlockSpec((1, D_MODEL), lambda i: (0, 0)),
                  pl.BlockSpec(wq.shape, lambda i: (0, 0)),
                  pl.BlockSpec(keys.shape, lambda i: (0, 0, 0))],
        out_specs=[pl.BlockSpec((D_MODEL, tm), lambda i: (0, i)), row_spec, row_spec, pair_spec, pair_spec],
        out_shape=[jax.ShapeDtypeStruct((D_MODEL, t), BF16), row_tab, row_tab, pair_tab, pair_tab],
        scratch_shapes=[pltpu.VMEM((A_ROWS, tm), F32) for _ in range(3)],
        compiler_params=_cparams(("parallel",)),
        name="peer_route",
    )(x2, norm_g.reshape(1, D_MODEL), wq, keys)


PEER_EC = SUBLANES * N_KEYS
PIPE_STAGES = 3
I_GROUP = 2


def _peer_expert_kernel(hnt_ref, u_ref, vt_ref, cnt_ref, e1z_ref, r2_ref, e2_ref, x_ref, out_ref,
                        acc_sc, s0_sc, s1_sc, p0_sc, p1_sc, *, ec):
    g = pl.program_id(1)
    tm = hnt_ref.shape[1]
    n_i = ec // N_KEYS

    def row_bf16(rows, k, ls):
        words = jnp.broadcast_to(rows[k:k + 1, ls], (N_KEYS // 2, LANES))
        return pltpu.bitcast(words, BF16)

    zero = jnp.zeros((), BF16)

    def gate_tile(ci, lt, s_in, p_out):
        ls = slice(lt * LANES, (lt + 1) * LANES)
        for i0 in range(0, n_i, I_GROUP):
            gates = [jnp.zeros((N_KEYS, LANES), BF16) for _ in range(I_GROUP)]
            for h in range(N_HEADS):
                r2t = pltpu.bitcast(r2_ref[h][:, ls], BF16)
                e2t = pltpu.bitcast(e2_ref[h][:, ls], BF16)
                tile0 = pl.multiple_of(ci * n_i + (i0 // SUBLANES) * SUBLANES, SUBLANES)
                cnt8 = cnt_ref[h, pl.ds(tile0, SUBLANES), :]
                e1z8 = e1z_ref[h, pl.ds(tile0, SUBLANES), :]
                for kk in range(I_GROUP):
                    k = i0 % SUBLANES + kk
                    sel = jnp.where(r2t < row_bf16(cnt8, k, ls), e2t, zero)
                    gates[kk] = gates[kk] + sel * row_bf16(e1z8, k, ls)
            for kk in range(I_GROUP):
                r0 = (i0 + kk) * N_KEYS
                s = s_in[r0:r0 + N_KEYS, ls].astype(BF16)
                act2 = s * (1.0 + lax.erf(s * (2.0 ** -0.5)))
                p_out[r0 // 2:(r0 + N_KEYS) // 2, ls] = pltpu.bitcast(gates[kk] * act2, jnp.uint32)

    def pipeline_step(c, u_blk, vt_blk, s_in, s_out, p_in, p_out, run_out, run_gate, run_pre):
        tiles_per_slab = MXU_LANES // LANES
        for sb in range(tm // MXU_LANES):
            cs = slice(sb * MXU_LANES, (sb + 1) * MXU_LANES)
            if run_out:
                acc_sc[:, cs] += jnp.dot(vt_blk, pltpu.bitcast(p_in[:, cs], BF16),
                                         preferred_element_type=F32)
            if run_gate:
                gate_tile(c - 1, sb * tiles_per_slab, s_in, p_out)
            if run_pre:
                s_out[:, cs] = jnp.dot(u_blk, hnt_ref[:, cs], preferred_element_type=F32)
            if run_gate:
                for lt in range(sb * tiles_per_slab + 1, (sb + 1) * tiles_per_slab):
                    gate_tile(c - 1, lt, s_in, p_out)

    def grid_step(first, last):
        even = (u_ref[:ec, :], vt_ref[:, :ec], s1_sc, s0_sc, p1_sc, p0_sc)
        odd = (u_ref[ec:, :], vt_ref[:, ec:], s0_sc, s1_sc, p0_sc, p1_sc)
        pipeline_step(2 * g, *even, run_out=not first, run_gate=not first, run_pre=not last)
        pipeline_step(2 * g + 1, *odd, run_out=not first, run_gate=not last, run_pre=not last)

    last_g = pl.num_programs(1) - 1

    @pl.when(g == 0)
    def _():
        acc_sc[...] = jnp.zeros_like(acc_sc)
        grid_step(True, False)

    @pl.when((g > 0) & (g < last_g))
    def _():
        grid_step(False, False)

    @pl.when(g == last_g)
    def _():
        grid_step(False, True)
        out_ref[...] = x_ref[...] + acc_sc[...].T


def _peer_expert(x2, hnt, u, vt, cnt, e1z, r2, e2, tm, ec):
    t = x2.shape[0]
    assert ec % (SUBLANES * N_KEYS) == 0 and (N_EXPERTS // ec) % 2 == 0
    n_chunks = N_EXPERTS // ec
    n_steps = (n_chunks + PIPE_STAGES - 1) // 2
    last_u = n_chunks // 2 - 1
    row_spec = pl.BlockSpec((N_HEADS, N_KEYS, tm), lambda i, g: (0, 0, i))
    pair_spec = pl.BlockSpec((N_HEADS, N_KEYS // 2, tm), lambda i, g: (0, 0, i))
    return pl.pallas_call(
        functools.partial(_peer_expert_kernel, ec=ec),
        grid=(t // tm, n_steps),
        in_specs=[pl.BlockSpec((D_MODEL, tm), lambda i, g: (0, i)),
                  pl.BlockSpec((2 * ec, D_MODEL), lambda i, g: (jnp.minimum(g, last_u), 0)),
                  pl.BlockSpec((D_MODEL, 2 * ec), lambda i, g: (0, jnp.maximum(g - 1, 0))),
                  row_spec, row_spec, pair_spec, pair_spec,
                  pl.BlockSpec((tm, D_MODEL), lambda i, g: (i, 0))],
        out_specs=pl.BlockSpec((tm, D_MODEL), lambda i, g: (i, 0)),
        out_shape=jax.ShapeDtypeStruct((t, D_MODEL), F32),
        scratch_shapes=[pltpu.VMEM((D_MODEL, tm), F32),
                        pltpu.VMEM((ec, tm), F32), pltpu.VMEM((ec, tm), F32),
                        pltpu.VMEM((ec // 2, tm), jnp.uint32), pltpu.VMEM((ec // 2, tm), jnp.uint32)],
        compiler_params=_cparams(("parallel", "arbitrary")),
        name="peer_expert",
    )(hnt, u, vt, cnt, e1z, r2, e2, x2)


def _peer_ffn(x2, norm_g, w_query, sub_keys, u_table, v_table, tm_route, tm_exp, ec):
    keys = sub_keys.reshape(2 * N_HEADS, N_KEYS, LANES).astype(BF16)
    hnt, cnt, e1z, r2, e2 = _peer_route(x2, norm_g, w_query.astype(BF16), keys, tm_route)
    return _peer_expert(x2, hnt, u_table.astype(BF16), v_table.T.astype(BF16), cnt, e1z, r2, e2, tm_exp, ec)


def kernel(x, positions, l0_norm_mix, l0_w_in, l0_b_gates, l0_head_norm, l0_w_out, l0_norm_ffn, l0_peer_wq, l0_peer_keys, l0_peer_u, l0_peer_v, l1_norm_mix, l1_w_qkv, l1_q_norm, l1_k_norm, l1_lambda_q1, l1_lambda_k1, l1_lambda_q2, l1_lambda_k2, l1_subln, l1_w_out, l1_norm_ffn, l1_peer_wq, l1_peer_keys, l1_peer_u, l1_peer_v):
    b, s, d = x.shape
    t = b * s
    tm = min(512, t)
    x2 = x.reshape(t, d)

    w_k, w_rows = _mlstm_layout(l0_w_in)
    k, qt, vt, ot, gt = _mlstm_proj(x2, l0_norm_mix, w_k, w_rows, tm)
    hf, hb = _mlstm(k, qt, vt, gt, l0_b_gates, b, s)
    x2 = _mlstm_out(x2, hf, hb, ot, l0_head_norm, l0_w_out.astype(BF16), tm)
    x2 = _peer_ffn(x2, l0_norm_ffn, l0_peer_wq, l0_peer_keys, l0_peer_u, l0_peer_v, min(256, t), tm, PEER_EC)

    lambda_init = 0.8 - 0.6 * math.exp(-0.3 * 1)
    pos = positions.reshape(t, 1).astype(F32)
    qa, ka, va = _da_proj(x2, l1_norm_mix, l1_w_qkv, l1_q_norm, l1_k_norm, pos, tm)
    to3 = lambda a: a.reshape(b, s, a.shape[-1])
    lam4 = jnp.stack([l1_lambda_q1, l1_lambda_k1, l1_lambda_q2, l1_lambda_k2]).astype(F32)
    att = _da_attn(to3(qa), to3(ka), to3(va), lam4, l1_subln, b, s, lambda_init, 512)
    x2 = _resid_matmul(x2, att.reshape(t, D_MODEL), l1_w_out.astype(BF16), tm)
    x2 = _peer_ffn(x2, l1_norm_ffn, l1_peer_wq, l1_peer_keys, l1_peer_u, l1_peer_v, min(256, t), tm, PEER_EC)
    return x2.reshape(b, s, d)
```

```python
import functools
import math

import jax
import jax.numpy as jnp
from jax import lax
from jax.experimental import pallas as pl
from jax.experimental.pallas import tpu as pltpu

F32 = jnp.float32
BF16 = jnp.bfloat16
HIGHEST = lax.Precision.HIGHEST

D_MODEL = 1024
EPS = 1e-6
LANES = 128
SUBLANES = 8
MXU_LANES = 256
N_HEADS = 8
ML_DQK = 64
ML_CHUNK = 256
ML_M_INIT = -1e30
DA_DHEAD = 64
ROPE_THETA = 500000.0
ROPE_HALF = 8
N_KEYS = 128
N_EXPERTS = N_KEYS * N_KEYS
PEER_TOPK = 16
VMEM_LIMIT_BYTES = 56 * 1024 * 1024


def _cparams(sem):
    return pltpu.CompilerParams(dimension_semantics=sem, vmem_limit_bytes=VMEM_LIMIT_BYTES)


def _rms_rows(x, g):
    ms = jnp.mean(x * x, axis=-1, keepdims=True)
    return x * lax.rsqrt(ms + EPS) * g


N_GATES = 4 * N_HEADS


def _mlstm_proj_kernel(x_ref, g_ref, wk_ref, wt_ref, k_ref, qt_ref, vt_ref, ot_ref, gt_ref):
    hn = _rms_rows(x_ref[...], g_ref[...]).astype(BF16)
    k_ref[...] = jnp.dot(hn, wk_ref[...], preferred_element_type=F32).astype(BF16)

    def feature_major(r0, n):
        return lax.dot_general(wt_ref[r0:r0 + n, :], hn, (((1,), (1,)), ((), ())), preferred_element_type=F32)

    qt_ref[...] = (feature_major(0, D_MODEL) * (ML_DQK ** -0.5)).astype(BF16)
    vt_ref[...] = feature_major(D_MODEL, D_MODEL).astype(BF16)
    ot_ref[...] = feature_major(2 * D_MODEL, D_MODEL)
    gt_ref[...] = feature_major(3 * D_MODEL, N_GATES)


def _mlstm_proj(x2, g, w_k, w_rows, tm):
    t = x2.shape[0]
    fm = lambda n, dt: (pl.BlockSpec((n, tm), lambda i: (0, i)), jax.ShapeDtypeStruct((n, t), dt))
    outs = [(pl.BlockSpec((tm, D_MODEL), lambda i: (i, 0)), jax.ShapeDtypeStruct((t, D_MODEL), BF16)),
            fm(D_MODEL, BF16), fm(D_MODEL, BF16), fm(D_MODEL, F32), fm(N_GATES, F32)]
    return pl.pallas_call(
        _mlstm_proj_kernel,
        grid=(t // tm,),
        in_specs=[pl.BlockSpec((tm, D_MODEL), lambda i: (i, 0)),
                  pl.BlockSpec((1, D_MODEL), lambda i: (0, 0)),
                  pl.BlockSpec(w_k.shape, lambda i: (0, 0)),
                  pl.BlockSpec(w_rows.shape, lambda i: (0, 0))],
        out_specs=[o[0] for o in outs],
        out_shape=[o[1] for o in outs],
        compiler_params=_cparams(("parallel",)),
        name="mlstm_proj",
    )(x2, g.reshape(1, D_MODEL), w_k, w_rows)


GATE_ROWS = ("u", "inter", "e_neg", "w", "s_prev", "s_loc")


def _mlstm_kernel(kf_ref, qtf_ref, vtf_ref, gtf_ref, gtf_next_ref, kb_ref, qtb_ref, vtb_ref, gtb_ref,
                  gtb_next_ref, bias_ref, mt_ref, of_ref, ob_ref, ct_sc, m_sc, pre_sc, rc_sc, *, L):
    c = pl.program_id(1)
    slot = lax.rem(c, 2)
    streams = ((0, kf_ref, qtf_ref, vtf_ref, gtf_ref, gtf_next_ref, of_ref),
               (1, kb_ref, qtb_ref, vtb_ref, gtb_ref, gtb_next_ref, ob_ref))

    def gate_math(d, gt_ref):
        return _mlstm_gate_math(d, gt_ref, bias_ref, mt_ref, m_sc[d, :, 0:1], L)

    def hand_over(d, to_slot, rows, r_cols, m_new):
        pre_sc[d, to_slot] = rows
        rc_sc[d, to_slot] = r_cols
        m_sc[d] = jnp.broadcast_to(m_new, (N_HEADS, LANES))

    @pl.when(c == 0)
    def _():
        ct_sc[...] = jnp.zeros_like(ct_sc)
        m_sc[...] = jnp.full_like(m_sc, ML_M_INIT)
        for d, _, _, _, gt_ref, _, _ in streams:
            hand_over(d, 0, *gate_math(d, gt_ref))

    for d, k_ref, qt_ref, vt_ref, _, gt_next_ref, o_ref in streams:
        nxt = gate_math(d, gt_next_ref)
        bits = lax.bitcast_convert_type(nxt[2], jnp.uint32)
        zero = ((bits >> 16) >> 16).astype(F32)
        _mlstm_heads(d, k_ref, qt_ref, vt_ref, mt_ref, pre_sc[d, slot], rc_sc[d, slot], zero, o_ref, ct_sc, L)
        hand_over(d, 1 - slot, *nxt)


def _mlstm_gate_math(d, gt_ref, bias_ref, mt_ref, m_prev, L):
    row0 = d * 2 * N_HEADS
    g = gt_ref[row0:row0 + 2 * N_HEADS, :] + bias_ref[row0:row0 + 2 * N_HEADS, :]
    li = g[:N_HEADS]
    gf = g[N_HEADS:]
    lf = jnp.minimum(gf, 0.0) - jnp.log1p(jnp.exp(-jnp.abs(gf)))
    mask_f = mt_ref[d]
    b = jnp.dot(lf, mask_f, precision=HIGHEST, preferred_element_type=F32)
    b_tot = jnp.min(b, axis=1, keepdims=True)
    r = li - b
    a = b_tot + r
    m_loc = jnp.max(a, axis=1, keepdims=True)
    w = jnp.exp(a - m_loc)
    g_in = b + m_prev
    allowed = mask_f > 0.5
    r_cols = jnp.concatenate([r, jnp.zeros((LANES - N_HEADS, L), F32)], axis=0).T
    c_max = jnp.concatenate(
        [jnp.max(jnp.where(allowed, r_cols[:, h:h + 1], -jnp.inf), axis=0, keepdims=True)
         for h in range(N_HEADS)], axis=0)
    m_t = jnp.maximum(g_in, b + c_max)
    m_new = jnp.maximum(b_tot + m_prev, m_loc)
    wide = lambda col: jnp.broadcast_to(col, (N_HEADS, L))
    rows = dict(u=b - m_t, inter=jnp.exp(g_in - m_t), e_neg=jnp.exp(-m_t), w=w,
                s_prev=wide(jnp.exp(b_tot + m_prev - m_new)), s_loc=wide(jnp.exp(m_loc - m_new)))
    return jnp.stack([rows[name] for name in GATE_ROWS]), r_cols, m_new


def _mlstm_heads(d, k_ref, qt_ref, vt_ref, mt_ref, gate_rows, r_cols, late_zero, o_ref, ct_sc, L):
    u, inter, e_neg, w, s_prev, s_loc = (gate_rows[i] for i in range(len(GATE_ROWS)))
    allowed = mt_ref[d] > 0.5
    ones_rows = (lax.broadcasted_iota(jnp.int32, (LANES, L), 0) == 0).astype(BF16)
    for h in range(N_HEADS):
        rows = slice(h * LANES, (h + 1) * LANES)
        hh = slice(h, h + 1)
        qt = qt_ref[rows, :]
        k = k_ref[:, rows]
        v_aug = jnp.concatenate([vt_ref[rows, :], ones_rows], axis=0)
        ct_prev = ct_sc[d, h]
        masked_r = jnp.where(allowed, r_cols[:, h:h + 1], -jnp.inf)
        u_h = u[hh, :] if h < N_HEADS // 2 else u[hh, :] + late_zero[hh, :]
        e_t = jnp.exp(masked_r + u_h)
        p_t = (jnp.dot(k, qt, preferred_element_type=F32) * e_t).astype(BF16)
        nd = (jnp.dot(v_aug, p_t, preferred_element_type=F32)
              + inter[hh, :] * jnp.dot(ct_prev.astype(BF16), qt, preferred_element_type=F32))
        den = jnp.maximum(jnp.abs(nd[LANES:LANES + 1, :]), e_neg[hh, :])
        o_ref[rows, :] = nd[:LANES, :] / den
        wv = (v_aug.astype(F32) * w[hh, :]).astype(BF16)
        ct_loc = jnp.dot(wv, k, preferred_element_type=F32)
        ct_sc[d, h] = s_prev[hh, 0:1] * ct_prev + s_loc[hh, 0:1] * ct_loc


def _mlstm(k, qt, vt, gt, bias, b, s):
    L = min(ML_CHUNK, s)
    nc = s // L
    tri = jnp.tril(jnp.ones((L, L), F32))
    masks_t = jnp.stack([tri.T, tri])
    bias_b = jnp.broadcast_to(bias.reshape(N_GATES, 1).astype(F32), (N_GATES, L))
    fwd = lambda bi, c: bi * nc + c
    bwd = lambda bi, c: bi * nc + nc - 1 - c
    fwd_next = lambda bi, c: bi * nc + jnp.minimum(c + 1, nc - 1)
    bwd_next = lambda bi, c: bi * nc + jnp.maximum(nc - 2 - c, 0)

    def seq_specs(pos, pos_next):
        fm = lambda n, p: pl.BlockSpec((n, L), lambda bi, c: (0, p(bi, c)))
        return [pl.BlockSpec((L, D_MODEL), lambda bi, c: (pos(bi, c), 0)), fm(D_MODEL, pos), fm(D_MODEL, pos),
                fm(N_GATES, pos), fm(N_GATES, pos_next)]

    out_spec = lambda pos: pl.BlockSpec((D_MODEL, L), lambda bi, c: (0, pos(bi, c)))
    out = jax.ShapeDtypeStruct((D_MODEL, b * s), F32)
    return pl.pallas_call(
        functools.partial(_mlstm_kernel, L=L),
        grid=(b, nc),
        in_specs=(seq_specs(fwd, fwd_next) + seq_specs(bwd, bwd_next)
                  + [pl.BlockSpec((N_GATES, L), lambda bi, c: (0, 0)),
                     pl.BlockSpec((2, L, L), lambda bi, c: (0, 0, 0))]),
        out_specs=[out_spec(fwd), out_spec(bwd)],
        out_shape=[out, out],
        scratch_shapes=[pltpu.VMEM((2, N_HEADS, 2 * LANES, LANES), F32),
                        pltpu.VMEM((2, N_HEADS, LANES), F32),
                        pltpu.VMEM((2, 2, len(GATE_ROWS), N_HEADS, L), F32),
                        pltpu.VMEM((2, 2, L, LANES), F32)],
        compiler_params=_cparams(("parallel", "arbitrary")),
        name="mlstm_scan",
    )(k, qt, vt, gt, gt, k, qt, vt, gt, gt, bias_b, masks_t)


def _mlstm_out_kernel(x_ref, hf_ref, hb_ref, ot_ref, hg_ref, w_ref, out_ref):
    tm = x_ref.shape[0]
    hs = hf_ref[...] + hb_ref[...]
    parts = []
    for h in range(N_HEADS):
        blk = hs[h * LANES:(h + 1) * LANES, :]
        ms = jnp.mean(blk * blk, axis=0, keepdims=True)
        parts.append(blk * lax.rsqrt(ms + EPS))
    gain = jnp.concatenate([hg_ref[...]] * (tm // LANES), axis=1)
    hn_t = (jnp.concatenate(parts, axis=0) * gain * jax.nn.sigmoid(ot_ref[...])).astype(BF16)
    out_ref[...] = x_ref[...] + lax.dot_general(hn_t, w_ref[...], (((0,), (0,)), ((), ())),
                                                preferred_element_type=F32)


def _mlstm_out(x2, hf, hb, ot, head_g, w_out, tm):
    t = x2.shape[0]
    gain = jnp.broadcast_to(head_g.reshape(D_MODEL, 1).astype(F32), (D_MODEL, LANES))
    return pl.pallas_call(
        _mlstm_out_kernel,
        grid=(t // tm,),
        in_specs=[pl.BlockSpec((tm, D_MODEL), lambda i: (i, 0)),
                  pl.BlockSpec((D_MODEL, tm), lambda i: (0, i)),
                  pl.BlockSpec((D_MODEL, tm), lambda i: (0, i)),
                  pl.BlockSpec((D_MODEL, tm), lambda i: (0, i)),
                  pl.BlockSpec((D_MODEL, LANES), lambda i: (0, 0)),
                  pl.BlockSpec((D_MODEL, D_MODEL), lambda i: (0, 0))],
        out_specs=pl.BlockSpec((tm, D_MODEL), lambda i: (i, 0)),
        out_shape=jax.ShapeDtypeStruct((t, D_MODEL), F32),
        compiler_params=_cparams(("parallel",)),
        name="mlstm_out",
    )(x2, hf, hb, ot, gain, w_out)


def _mlstm_layout(w_in):
    nq = N_HEADS * ML_DQK
    pad_heads = lambda w: jnp.pad(w.reshape(D_MODEL, N_HEADS, ML_DQK),
                                  ((0, 0), (0, 0), (0, LANES - ML_DQK))).reshape(D_MODEL, N_HEADS * LANES)
    w_q = pad_heads(w_in[:, :nq])
    w_k = pad_heads(w_in[:, nq:2 * nq])
    rest = w_in[:, 2 * nq:]
    w_rows = jnp.concatenate([w_q, rest], axis=1).T
    return w_k.astype(BF16), w_rows.astype(BF16)


def _da_proj_kernel(x_ref, g_ref, w_ref, pos_ref, inv_ref, wts_ref, gq_ref, gk_ref,
                    cq_ref, ck_ref, q_ref, k_ref, v_ref):
    hn = _rms_rows(x_ref[...], g_ref[...]).astype(BF16)
    ang = pos_ref[...] * inv_ref[...]
    cos = jnp.cos(ang)
    sin = jnp.sin(ang)
    trig_q = cq_ref[0:1, :] * cos + cq_ref[1:2, :] * sin + cq_ref[2:3, :]
    trig_k = ck_ref[0:1, :] * cos + ck_ref[1:2, :] * sin + ck_ref[2:3, :]
    fq = trig_q * gq_ref[...]
    fk = trig_k * gk_ref[...]
    nsub = 2 * N_HEADS
    per_dot = MXU_LANES // LANES
    for u0 in range(0, nsub, per_dot):
        for off, fac, wts, o_ref in ((0, fq, wts_ref[0:1, :], q_ref), (nsub * LANES, fk, wts_ref[1:2, :], k_ref)):
            raw2 = jnp.dot(hn, w_ref[:, off + u0 * LANES: off + (u0 + per_dot) * LANES],
                           preferred_element_type=F32)
            for du in range(per_dot):
                raw = raw2[:, du * LANES:(du + 1) * LANES]
                ms = jnp.sum(raw * raw * wts, axis=-1, keepdims=True) * (1.0 / DA_DHEAD)
                o_ref[:, (u0 + du) * LANES:(u0 + du + 1) * LANES] = (raw * lax.rsqrt(ms + EPS) * fac).astype(BF16)
    v_ref[...] = jnp.dot(hn, w_ref[:, 2 * nsub * LANES:], preferred_element_type=F32).astype(BF16)


def _da_layout(w_qkv, q_g, k_g):
    x1 = list(range(0, 8))
    x2 = list(range(8, 16))
    rest = list(range(16, 64))
    q_groups = [(x1, 1, 0, 0), (x1, 1, 0, 0), (x2, 0, -1, 0), (x2, 0, -1, 0),
                (x2, 1, 0, 0), (x2, 1, 0, 0), (x1, 0, 1, 0), (x1, 0, 1, 0)]
    k_groups = [(x1, 1, 0, 0), (x2, 0, -1, 0), (x1, 1, 0, 0), (x2, 0, -1, 0),
                (x2, 1, 0, 0), (x1, 0, 1, 0), (x2, 1, 0, 0), (x1, 0, 1, 0)]

    def tables(groups):
        src, coef = [], []
        for dims, a, b_, c_ in groups:
            src += dims
            coef += [(a, b_, c_)] * len(dims)
        src += rest
        coef += [(0, 0, 1)] * len(rest)
        pad = LANES - len(src)
        valid = [1.0] * len(src) + [0.0] * pad
        src += [0] * pad
        coef += [(0, 0, 0)] * pad
        return jnp.array(src, jnp.int32), jnp.array(coef, F32).T, jnp.array(valid, F32)

    src_q, coef_q, valid_q = tables(q_groups)
    src_k, coef_k, valid_k = tables(k_groups)
    nsub = 2 * N_HEADS
    wq = w_qkv[:, :nsub * DA_DHEAD].reshape(D_MODEL, nsub, DA_DHEAD)
    wk = w_qkv[:, nsub * DA_DHEAD:2 * nsub * DA_DHEAD].reshape(D_MODEL, nsub, DA_DHEAD)
    wv = w_qkv[:, 2 * nsub * DA_DHEAD:]
    wq_p = (wq[:, :, src_q] * valid_q).reshape(D_MODEL, nsub * LANES)
    wk_p = (wk[:, :, src_k] * valid_k).reshape(D_MODEL, nsub * LANES)
    w_cat = jnp.concatenate([wq_p, wk_p, wv], axis=1).astype(BF16)
    lanes = jnp.arange(LANES)
    plain = (lanes >= 64) & (lanes < 112)
    wts_q = ((lanes < 8) | ((lanes >= 16) & (lanes < 24)) | plain).astype(F32)
    wts_k = ((lanes < 16) | plain).astype(F32)
    wts = jnp.concatenate([wts_q[None], wts_k[None], jnp.zeros((6, LANES), F32)], axis=0)
    gq = (q_g[src_q] * valid_q).reshape(1, LANES) * (DA_DHEAD ** -0.5 * math.log2(math.e))
    gk = (k_g[src_k] * valid_k).reshape(1, LANES)
    freq = jnp.where(lanes < 64, lanes % ROPE_HALF, 0)
    inv = ROPE_THETA ** (-freq.astype(F32) * 2.0 / (2 * ROPE_HALF))
    inv = jnp.where(lanes < 64, inv, 0.0).reshape(1, LANES)
    pad8 = lambda a: jnp.concatenate([a, jnp.zeros((5, LANES), F32)], axis=0)
    return w_cat, inv, wts, gq, gk, pad8(coef_q), pad8(coef_k)


def _da_proj(x2, norm_g, w_qkv, q_g, k_g, pos, tm):
    t = x2.shape[0]
    w_cat, inv, wts, gq, gk, cq, ck = _da_layout(w_qkv, q_g, k_g)
    nq = 2 * N_HEADS * LANES
    const = lambda shp: pl.BlockSpec(shp, lambda i: (0,) * len(shp))
    return pl.pallas_call(
        _da_proj_kernel,
        grid=(t // tm,),
        in_specs=[pl.BlockSpec((tm, D_MODEL), lambda i: (i, 0)),
                  const((1, D_MODEL)), const(w_cat.shape),
                  pl.BlockSpec((tm, 1), lambda i: (i, 0)),
                  const((1, LANES)), const((8, LANES)), const((1, LANES)), const((1, LANES)),
                  const((8, LANES)), const((8, LANES))],
        out_specs=[pl.BlockSpec((tm, nq), lambda i: (i, 0)),
                   pl.BlockSpec((tm, nq), lambda i: (i, 0)),
                   pl.BlockSpec((tm, D_MODEL), lambda i: (i, 0))],
        out_shape=[jax.ShapeDtypeStruct((t, nq), BF16),
                   jax.ShapeDtypeStruct((t, nq), BF16),
                   jax.ShapeDtypeStruct((t, D_MODEL), BF16)],
        compiler_params=_cparams(("parallel",)),
        name="da_proj",
    )(x2, norm_g.reshape(1, D_MODEL), w_cat, pos, inv, wts, gq, gk, cq, ck)


def _da_attn_kernel(q_ref, k_ref, v_ref, lam_ref, sg_ref, o_ref, s_sc, *, lambda_init, kb):
    out_scale = 1.0 - lambda_init
    lam_v = lam_ref[...]
    lam = (jnp.exp(jnp.sum(lam_v[0:1] * lam_v[1:2], axis=-1, keepdims=True))
           - jnp.exp(jnp.sum(lam_v[2:3] * lam_v[3:4], axis=-1, keepdims=True))
           + lambda_init)
    tq = q_ref.shape[0]
    th = tq // 2
    n_blk = k_ref.shape[0] // kb
    subs = (0, 1)

    ones_blk = (lax.broadcasted_iota(jnp.int32, (kb, LANES), 1) == 0).astype(BF16)

    def score_block(qs, r0, blk, m_part):
        rows = slice(blk * kb, (blk + 1) * kb)
        for c in subs:
            s = lax.dot_general(qs[c], k_ref[rows, c * LANES:(c + 1) * LANES], (((1,), (1,)), ((), ())),
                                preferred_element_type=F32)
            s_sc[c, r0:r0 + th, rows] = s
            for j in range(kb // LANES):
                m_part[c] = jnp.maximum(m_part[c], s[:, j * LANES:(j + 1) * LANES])

    def attend_block(m, r0, blk, acc):
        rows = slice(blk * kb, (blk + 1) * kb)
        v_aug = jnp.concatenate([v_ref[rows, :], ones_blk], axis=1)
        for c in subs:
            p = jnp.exp2(s_sc[c, r0:r0 + th, rows] - m[c])
            acc[c] = acc[c] + jnp.dot(p.astype(BF16), v_aug, preferred_element_type=F32)

    def finish(r0, acc):
        outs = [acc[c][:, :LANES] / acc[c][:, LANES:LANES + 1] for c in subs]
        o = outs[0] - lam * outs[1]
        ms = jnp.mean(o * o, axis=-1, keepdims=True)
        o_ref[r0:r0 + th, :] = ((o * lax.rsqrt(ms + EPS) * sg_ref[...]) * out_scale).astype(o_ref.dtype)

    new_max = lambda: [jnp.full((th, LANES), -jnp.inf, F32) for _ in subs]
    new_acc = lambda: [jnp.zeros((th, 2 * LANES), F32) for _ in subs]
    row_max = lambda m_part: [jnp.max(m_part[c], axis=-1, keepdims=True) for c in subs]

    q_a = [q_ref[:th, c * LANES:(c + 1) * LANES] for c in subs]
    mp_a, acc_a = new_max(), new_acc()
    for blk in range(n_blk):
        score_block(q_a, 0, blk, mp_a)
    m_a = row_max(mp_a)
    bits = lax.bitcast_convert_type(m_a[0] + m_a[1], jnp.uint32)
    zero = ((bits >> 16) >> 16).astype(F32)
    q_b = [(q_ref[th:, c * LANES:(c + 1) * LANES].astype(F32) + zero).astype(BF16) for c in subs]
    mp_b, acc_b = new_max(), new_acc()
    for blk in range(n_blk):
        score_block(q_b, th, blk, mp_b)
        attend_block(m_a, 0, blk, acc_a)
    finish(0, acc_a)
    m_b = row_max(mp_b)
    for blk in range(n_blk):
        attend_block(m_b, th, blk, acc_b)
    finish(th, acc_b)


DA_KEY_BLOCK = 1024


def _da_attn(q, k, v, lam4, subln_g, b, s, lambda_init, tq):
    tq = min(tq, s)
    kb = min(DA_KEY_BLOCK, s)
    return pl.pallas_call(
        functools.partial(_da_attn_kernel, lambda_init=lambda_init, kb=kb),
        grid=(b, N_HEADS, s // tq),
        in_specs=[pl.BlockSpec((None, tq, 2 * LANES), lambda bi, h, i: (bi, i, h)),
                  pl.BlockSpec((None, s, 2 * LANES), lambda bi, h, i: (bi, 0, h)),
                  pl.BlockSpec((None, s, LANES), lambda bi, h, i: (bi, 0, h)),
                  pl.BlockSpec((4, DA_DHEAD), lambda bi, h, i: (0, 0)),
                  pl.BlockSpec((1, LANES), lambda bi, h, i: (0, 0))],
        out_specs=pl.BlockSpec((None, tq, LANES), lambda bi, h, i: (bi, i, h)),
        out_shape=jax.ShapeDtypeStruct((b, s, D_MODEL), BF16),
        scratch_shapes=[pltpu.VMEM((2, tq, s), F32)],
        compiler_params=_cparams(("parallel", "parallel", "arbitrary")),
        name="da_attn",
    )(q, k, v, lam4, subln_g.reshape(1, LANES))


def _resid_matmul_kernel(x_ref, a_ref, w_ref, out_ref):
    out_ref[...] = x_ref[...] + jnp.dot(a_ref[...], w_ref[...], preferred_element_type=F32)


def _resid_matmul(x2, a2, w, tm):
    t = x2.shape[0]
    return pl.pallas_call(
        _resid_matmul_kernel,
        grid=(t // tm,),
        in_specs=[pl.BlockSpec((tm, D_MODEL), lambda i: (i, 0)),
                  pl.BlockSpec((tm, D_MODEL), lambda i: (i, 0)),
                  pl.BlockSpec((D_MODEL, D_MODEL), lambda i: (0, 0))],
        out_specs=pl.BlockSpec((tm, D_MODEL), lambda i: (i, 0)),
        out_shape=jax.ShapeDtypeStruct((t, D_MODEL), F32),
        compiler_params=_cparams(("parallel",)),
        name="resid_matmul",
    )(x2, a2, w)


N_EXTRACT = PEER_TOPK + 1
A_ROWS = 24
N_FULL_ROWS = 8


RANK_NONE = 64.0


def _oddeven_merge_sort_pairs(n):
    pairs = []

    def merge(lo, m, r):
        step = 2 * r
        if step < m:
            merge(lo, m, step)
            merge(lo + r, m, step)
            pairs.extend((i, i + r) for i in range(lo + r, lo + m - r, step))
        else:
            pairs.append((lo, lo + r))

    def sort(lo, m):
        if m > 1:
            sort(lo, m // 2)
            sort(lo + m // 2, m // 2)
            merge(lo, m, 1)

    sort(0, n)
    return pairs


def _extract_desc(vals, a_sc):
    n_grp = vals.shape[0] // SUBLANES
    for lt in range(vals.shape[1] // LANES):
        ls = slice(lt * LANES, (lt + 1) * LANES)
        col = [vals[v * SUBLANES:(v + 1) * SUBLANES, ls] for v in range(n_grp)]
        for i, j in _oddeven_merge_sort_pairs(n_grp):
            col[i], col[j] = jnp.maximum(col[i], col[j]), jnp.minimum(col[i], col[j])
        for t in range(N_EXTRACT):
            mx = jnp.max(col[0], axis=0, keepdims=True)
            a_sc[t:t + 1, ls] = mx
            remaining = N_EXTRACT - 1 - t
            if remaining:
                hit = col[0] == mx
                for v in range(min(remaining, n_grp)):
                    below = col[v + 1] if v + 1 < n_grp else -jnp.inf
                    col[v] = jnp.where(hit, below, col[v])


def _ranks_of(vals, a):
    rank = jnp.full(vals.shape, RANK_NONE, F32)
    for qq in range(N_EXTRACT):
        rank = jnp.where(vals == a[qq:qq + 1, :], float(qq), rank)
    return rank


def _dup_bf16_words(x):
    hi = lax.bitcast_convert_type(x.astype(BF16).astype(F32), jnp.uint32)
    return hi | (hi >> 16)


def _peer_route_kernel(x_ref, g_ref, wq_ref, keys_ref, hnt_ref, cnt_ref, e1z_ref, r2_ref, e2_ref,
                       a1_sc, a2_sc, c_sc):
    tm = x_ref.shape[0]
    hn = _rms_rows(x_ref[...], g_ref[...])
    hnt_ref[...] = hn.T.astype(BF16)
    q = jnp.dot(hn.astype(BF16), wq_ref[...], preferred_element_type=F32).astype(BF16)
    neg = jnp.full((A_ROWS, tm), -jnp.inf, F32)
    for h in range(N_HEADS):
        s_t = []
        for p, a_sc in ((0, a1_sc), (1, a2_sc)):
            hp = 2 * h + p
            st = lax.dot_general(keys_ref[hp], q[:, hp * LANES:(hp + 1) * LANES],
                                 (((1,), (1,)), ((), ())), preferred_element_type=F32)
            a_sc[...] = neg
            _extract_desc(st, a_sc)
            s_t.append(st)
        a1 = a1_sc[...]
        a2 = a2_sc[...]
        rank2 = _ranks_of(s_t[1], a2)
        blocks = [a1[0:1, :] + a2]
        blocks += [a1[p:p + 1, :] + a2[0:SUBLANES, :] for p in range(1, N_FULL_ROWS)]
        blocks.append(a1[N_FULL_ROWS:, :] + a2[0:1, :])
        blocks.append(jnp.full((N_KEYS - sum(blk.shape[0] for blk in blocks), tm), -jnp.inf, F32))
        c_sc[...] = neg
        _extract_desc(jnp.concatenate(blocks, axis=0), c_sc)
        cv = c_sc[...]
        topk_rows = lax.broadcasted_iota(jnp.int32, (A_ROWS, tm), 0) < PEER_TOPK
        z = jnp.sum(jnp.where(topk_rows, jnp.exp(cv - cv[0:1, :]), 0.0), axis=0, keepdims=True)
        tau = 0.5 * (cv[PEER_TOPK - 1:PEER_TOPK, :] + cv[PEER_TOPK:PEER_TOPK + 1, :])
        s1, s2 = s_t
        thr_rank = tau - a1
        cnt_rank = jnp.zeros_like(a1)
        for qq in range(N_EXTRACT):
            cnt_rank = cnt_rank + jnp.where(a2[qq:qq + 1, :] >= thr_rank, 1.0, 0.0)
        cnt = jnp.zeros_like(s1)
        for pp in range(N_EXTRACT):
            cnt = jnp.where(s1 == a1[pp:pp + 1, :], cnt_rank[pp:pp + 1, :], cnt)
        cnt_ref[h] = _dup_bf16_words(cnt)
        e1z_ref[h] = _dup_bf16_words(jnp.exp(s1 - a1[0:1, :]) * (0.5 / z))
        r2_ref[h] = pltpu.bitcast(rank2.astype(BF16), jnp.uint32)
        e2_ref[h] = pltpu.bitcast(jnp.exp(s2 - a2[0:1, :]).astype(BF16), jnp.uint32)


def _peer_route(x2, norm_g, wq, keys, tm):
    t = x2.shape[0]
    row_spec = pl.BlockSpec((N_HEADS, N_KEYS, tm), lambda i: (0, 0, i))
    pair_spec = pl.BlockSpec((N_HEADS, N_KEYS // 2, tm), lambda i: (0, 0, i))
    row_tab = jax.ShapeDtypeStruct((N_HEADS, N_KEYS, t), jnp.uint32)
    pair_tab = jax.ShapeDtypeStruct((N_HEADS, N_KEYS // 2, t), jnp.uint32)
    return pl.pallas_call(
        _peer_route_kernel,
        grid=(t // tm,),
        in_specs=[pl.BlockSpec((tm, D_MODEL), lambda i: (i, 0)),
                  pl.BlockSpec((1, D_MODEL), lambda i: (0, 0)),
                  pl.BlockSpec(wq.shape, lambda i: (0, 0)),
                  pl.BlockSpec(keys.shape, lambda i: (0, 0, 0))],
        out_specs=[pl.BlockSpec((D_MODEL, tm), lambda i: (0, i)), row_spec, row_spec, pair_spec, pair_spec],
        out_shape=[jax.ShapeDtypeStruct((D_MODEL, t), BF16), row_tab, row_tab, pair_tab, pair_tab],
        scratch_shapes=[pltpu.VMEM((A_ROWS, tm), F32) for _ in range(3)],
        compiler_params=_cparams(("parallel",)),
        name="peer_route",
    )(x2, norm_g.reshape(1, D_MODEL), wq, keys)


PEER_EC = SUBLANES * N_KEYS
PIPE_STAGES = 3
I_GROUP = 2


def _peer_expert_kernel(hnt_ref, u_ref, vt_ref, cnt_ref, e1z_ref, r2_ref, e2_ref, x_ref, out_ref,
                        acc_sc, s0_sc, s1_sc, p0_sc, p1_sc, *, ec):
    g = pl.program_id(1)
    tm = hnt_ref.shape[1]
    n_i = ec // N_KEYS

    def row_bf16(rows, k, ls):
        words = jnp.broadcast_to(rows[k:k + 1, ls], (N_KEYS // 2, LANES))
        return pltpu.bitcast(words, BF16)

    zero = jnp.zeros((), BF16)

    def gate_tile(ci, lt, s_in, p_out):
        ls = slice(lt * LANES, (lt + 1) * LANES)
        for i0 in range(0, n_i, I_GROUP):
            gates = [jnp.zeros((N_KEYS, LANES), BF16) for _ in range(I_GROUP)]
            for h in range(N_HEADS):
                r2t = pltpu.bitcast(r2_ref[h][:, ls], BF16)
                e2t = pltpu.bitcast(e2_ref[h][:, ls], BF16)
                tile0 = pl.multiple_of(ci * n_i + (i0 // SUBLANES) * SUBLANES, SUBLANES)
                cnt8 = cnt_ref[h, pl.ds(tile0, SUBLANES), :]
                e1z8 = e1z_ref[h, pl.ds(tile0, SUBLANES), :]
                for kk in range(I_GROUP):
                    k = i0 % SUBLANES + kk
                    sel = jnp.where(r2t < row_bf16(cnt8, k, ls), e2t, zero)
                    gates[kk] = gates[kk] + sel * row_bf16(e1z8, k, ls)
            for kk in range(I_GROUP):
                r0 = (i0 + kk) * N_KEYS
                s = s_in[r0:r0 + N_KEYS, ls].astype(BF16)
                act2 = s * (1.0 + lax.erf(s * (2.0 ** -0.5)))
                p_out[r0 // 2:(r0 + N_KEYS) // 2, ls] = pltpu.bitcast(gates[kk] * act2, jnp.uint32)

    def pipeline_step(c, u_blk, vt_blk, s_in, s_out, p_in, p_out, run_out, run_gate, run_pre):
        tiles_per_slab = MXU_LANES // LANES
        for sb in range(tm // MXU_LANES):
            cs = slice(sb * MXU_LANES, (sb + 1) * MXU_LANES)
            if run_out:
                acc_sc[:, cs] += jnp.dot(vt_blk, pltpu.bitcast(p_in[:, cs], BF16),
                                         preferred_element_type=F32)
            if run_gate:
                gate_tile(c - 1, sb * tiles_per_slab, s_in, p_out)
            if run_pre:
                s_out[:, cs] = jnp.dot(u_blk, hnt_ref[:, cs], preferred_element_type=F32)
            if run_gate:
                for lt in range(sb * tiles_per_slab + 1, (sb + 1) * tiles_per_slab):
                    gate_tile(c - 1, lt, s_in, p_out)

    def grid_step(first, last):
        even = (u_ref[:ec, :], vt_ref[:, :ec], s1_sc, s0_sc, p1_sc, p0_sc)
        odd = (u_ref[ec:, :], vt_ref[:, ec:], s0_sc, s1_sc, p0_sc, p1_sc)
        pipeline_step(2 * g, *even, run_out=not first, run_gate=not first, run_pre=not last)
        pipeline_step(2 * g + 1, *odd, run_out=not first, run_gate=not last, run_pre=not last)

    last_g = pl.num_programs(1) - 1

    @pl.when(g == 0)
    def _():
        acc_sc[...] = jnp.zeros_like(acc_sc)
        grid_step(True, False)

    @pl.when((g > 0) & (g < last_g))
    def _():
        grid_step(False, False)

    @pl.when(g == last_g)
    def _():
        grid_step(False, True)
        out_ref[...] = x_ref[...] + acc_sc[...].T


def _peer_expert(x2, hnt, u, vt, cnt, e1z, r2, e2, tm, ec):
    t = x2.shape[0]
    assert ec % (SUBLANES * N_KEYS) == 0 and (N_EXPERTS // ec) % 2 == 0
    n_chunks = N_EXPERTS // ec
    n_steps = (n_chunks + PIPE_STAGES - 1) // 2
    last_u = n_chunks // 2 - 1
    row_spec = pl.BlockSpec((N_HEADS, N_KEYS, tm), lambda i, g: (0, 0, i))
    pair_spec = pl.BlockSpec((N_HEADS, N_KEYS // 2, tm), lambda i, g: (0, 0, i))
    return pl.pallas_call(
        functools.partial(_peer_expert_kernel, ec=ec),
        grid=(t // tm, n_steps),
        in_specs=[pl.BlockSpec((D_MODEL, tm), lambda i, g: (0, i)),
                  pl.BlockSpec((2 * ec, D_MODEL), lambda i, g: (jnp.minimum(g, last_u), 0)),
                  pl.BlockSpec((D_MODEL, 2 * ec), lambda i, g: (0, jnp.maximum(g - 1, 0))),
                  row_spec, row_spec, pair_spec, pair_spec,
                  pl.BlockSpec((tm, D_MODEL), lambda i, g: (i, 0))],
        out_specs=pl.BlockSpec((tm, D_MODEL), lambda i, g: (i, 0)),
        out_shape=jax.ShapeDtypeStruct((t, D_MODEL), F32),
        scratch_shapes=[pltpu.VMEM((D_MODEL, tm), F32),
                        pltpu.VMEM((ec, tm), F32), pltpu.VMEM((ec, tm), F32),
                        pltpu.VMEM((ec // 2, tm), jnp.uint32), pltpu.VMEM((ec // 2, tm), jnp.uint32)],
        compiler_params=_cparams(("parallel", "arbitrary")),
        name="peer_expert",
    )(hnt, u, vt, cnt, e1z, r2, e2, x2)


def _peer_ffn(x2, norm_g, w_query, sub_keys, u_table, v_table, tm_route, tm_exp, ec):
    keys = sub_keys.reshape(2 * N_HEADS, N_KEYS, LANES).astype(BF16)
    hnt, cnt, e1z, r2, e2 = _peer_route(x2, norm_g, w_query.astype(BF16), keys, tm_route)
    return _peer_expert(x2, hnt, u_table.astype(BF16), v_table.T.astype(BF16), cnt, e1z, r2, e2, tm_exp, ec)


def kernel(x, positions, l0_norm_mix, l0_w_in, l0_b_gates, l0_head_norm, l0_w_out, l0_norm_ffn, l0_peer_wq, l0_peer_keys, l0_peer_u, l0_peer_v, l1_norm_mix, l1_w_qkv, l1_q_norm, l1_k_norm, l1_lambda_q1, l1_lambda_k1, l1_lambda_q2, l1_lambda_k2, l1_subln, l1_w_out, l1_norm_ffn, l1_peer_wq, l1_peer_keys, l1_peer_u, l1_peer_v):
    b, s, d = x.shape
    t = b * s
    tm = min(512, t)
    x2 = x.reshape(t, d)

    w_k, w_rows = _mlstm_layout(l0_w_in)
    k, qt, vt, ot, gt = _mlstm_proj(x2, l0_norm_mix, w_k, w_rows, tm)
    hf, hb = _mlstm(k, qt, vt, gt, l0_b_gates, b, s)
    x2 = _mlstm_out(x2, hf, hb, ot, l0_head_norm, l0_w_out.astype(BF16), tm)
    x2 = _peer_ffn(x2, l0_norm_ffn, l0_peer_wq, l0_peer_keys, l0_peer_u, l0_peer_v, min(256, t), tm, PEER_EC)

    lambda_init = 0.8 - 0.6 * math.exp(-0.3 * 1)
    pos = positions.reshape(t, 1).astype(F32)
    qa, ka, va = _da_proj(x2, l1_norm_mix, l1_w_qkv, l1_q_norm, l1_k_norm, pos, tm)
    to3 = lambda a: a.reshape(b, s, a.shape[-1])
    lam4 = jnp.stack([l1_lambda_q1, l1_lambda_k1, l1_lambda_q2, l1_lambda_k2]).astype(F32)
    att = _da_attn(to3(qa), to3(ka), to3(va), lam4, l1_subln, b, s, lambda_init, 512)
    x2 = _resid_matmul(x2, att.reshape(t, D_MODEL), l1_w_out.astype(BF16), tm)
    x2 = _peer_ffn(x2, l1_norm_ffn, l1_peer_wq, l1_peer_keys, l1_peer_u, l1_peer_v, min(256, t), tm, PEER_EC)
    return x2.reshape(b, s, d)
```

```python
import functools
import math

import jax
import jax.numpy as jnp
from jax import lax
from jax.experimental import pallas as pl
from jax.experimental.pallas import tpu as pltpu

F32 = jnp.float32
BF16 = jnp.bfloat16
HIGHEST = lax.Precision.HIGHEST

D_MODEL = 1024
EPS = 1e-6
LANES = 128
SUBLANES = 8
MXU_LANES = 256
N_HEADS = 8
ML_DQK = 64
ML_CHUNK = 256
ML_M_INIT = -1e30
DA_DHEAD = 64
ROPE_THETA = 500000.0
ROPE_HALF = 8
N_KEYS = 128
N_EXPERTS = N_KEYS * N_KEYS
PEER_TOPK = 16
VMEM_LIMIT_BYTES = 56 * 1024 * 1024
TOKEN_TILE = 512
ROUTE_TILE = 256
DA_QUERY_TILE = 512


def _cparams(sem):
    return pltpu.CompilerParams(dimension_semantics=sem, vmem_limit_bytes=VMEM_LIMIT_BYTES)


def _rms_rows(x, g):
    ms = jnp.mean(x * x, axis=-1, keepdims=True)
    return x * lax.rsqrt(ms + EPS) * g


N_GATES = 4 * N_HEADS


def _mlstm_proj_kernel(x_ref, g_ref, wk_ref, wt_ref, k_ref, qt_ref, vt_ref, ot_ref, gt_ref):
    hn = _rms_rows(x_ref[...], g_ref[...]).astype(BF16)
    k_ref[...] = jnp.dot(hn, wk_ref[...], preferred_element_type=F32).astype(BF16)

    def feature_major(r0, n):
        return lax.dot_general(wt_ref[r0:r0 + n, :], hn, (((1,), (1,)), ((), ())), preferred_element_type=F32)

    qt_ref[...] = (feature_major(0, D_MODEL) * (ML_DQK ** -0.5)).astype(BF16)
    vt_ref[...] = feature_major(D_MODEL, D_MODEL).astype(BF16)
    ot_ref[...] = feature_major(2 * D_MODEL, D_MODEL)
    gt_ref[...] = feature_major(3 * D_MODEL, N_GATES)


def _mlstm_proj(x2, g, w_k, w_rows, tm):
    t = x2.shape[0]
    fm = lambda n, dt: (pl.BlockSpec((n, tm), lambda i: (0, i)), jax.ShapeDtypeStruct((n, t), dt))
    outs = [(pl.BlockSpec((tm, D_MODEL), lambda i: (i, 0)), jax.ShapeDtypeStruct((t, D_MODEL), BF16)),
            fm(D_MODEL, BF16), fm(D_MODEL, BF16), fm(D_MODEL, F32), fm(N_GATES, F32)]
    return pl.pallas_call(
        _mlstm_proj_kernel,
        grid=(t // tm,),
        in_specs=[pl.BlockSpec((tm, D_MODEL), lambda i: (i, 0)),
                  pl.BlockSpec((1, D_MODEL), lambda i: (0, 0)),
                  pl.BlockSpec(w_k.shape, lambda i: (0, 0)),
                  pl.BlockSpec(w_rows.shape, lambda i: (0, 0))],
        out_specs=[o[0] for o in outs],
        out_shape=[o[1] for o in outs],
        compiler_params=_cparams(("parallel",)),
        name="mlstm_proj",
    )(x2, g.reshape(1, D_MODEL), w_k, w_rows)


GATE_ROWS = ("u", "inter", "e_neg", "w", "s_prev", "s_loc")


def _mlstm_kernel(kf_ref, qtf_ref, vtf_ref, gtf_ref, gtf_next_ref, kb_ref, qtb_ref, vtb_ref, gtb_ref,
                  gtb_next_ref, bias_ref, mt_ref, of_ref, ob_ref, ct_sc, m_sc, pre_sc, rc_sc, *, L):
    c = pl.program_id(1)
    slot = lax.rem(c, 2)
    streams = ((0, kf_ref, qtf_ref, vtf_ref, gtf_ref, gtf_next_ref, of_ref),
               (1, kb_ref, qtb_ref, vtb_ref, gtb_ref, gtb_next_ref, ob_ref))

    def gate_math(d, gt_ref):
        return _mlstm_gate_math(d, gt_ref, bias_ref, mt_ref, m_sc[d, :, 0:1], L)

    def hand_over(d, to_slot, rows, r_cols, m_new):
        pre_sc[d, to_slot] = rows
        rc_sc[d, to_slot] = r_cols
        m_sc[d] = jnp.broadcast_to(m_new, (N_HEADS, LANES))

    @pl.when(c == 0)
    def _():
        ct_sc[...] = jnp.zeros_like(ct_sc)
        m_sc[...] = jnp.full_like(m_sc, ML_M_INIT)
        for d, _, _, _, gt_ref, _, _ in streams:
            hand_over(d, 0, *gate_math(d, gt_ref))

    for d, k_ref, qt_ref, vt_ref, _, gt_next_ref, o_ref in streams:
        nxt = gate_math(d, gt_next_ref)
        bits = lax.bitcast_convert_type(nxt[2], jnp.uint32)
        zero = ((bits >> 16) >> 16).astype(F32)
        _mlstm_heads(d, k_ref, qt_ref, vt_ref, mt_ref, pre_sc[d, slot], rc_sc[d, slot], zero, o_ref, ct_sc, L)
        hand_over(d, 1 - slot, *nxt)


def _mlstm_gate_math(d, gt_ref, bias_ref, mt_ref, m_prev, L):
    row0 = d * 2 * N_HEADS
    g = gt_ref[row0:row0 + 2 * N_HEADS, :] + bias_ref[row0:row0 + 2 * N_HEADS, :]
    li = g[:N_HEADS]
    gf = g[N_HEADS:]
    lf = jnp.minimum(gf, 0.0) - jnp.log1p(jnp.exp(-jnp.abs(gf)))
    mask_f = mt_ref[d]
    b = jnp.dot(lf, mask_f, precision=HIGHEST, preferred_element_type=F32)
    b_tot = jnp.min(b, axis=1, keepdims=True)
    r = li - b
    a = b_tot + r
    m_loc = jnp.max(a, axis=1, keepdims=True)
    w = jnp.exp(a - m_loc)
    g_in = b + m_prev
    allowed = mask_f > 0.5
    r_cols = jnp.concatenate([r, jnp.zeros((LANES - N_HEADS, L), F32)], axis=0).T
    c_max = jnp.concatenate(
        [jnp.max(jnp.where(allowed, r_cols[:, h:h + 1], -jnp.inf), axis=0, keepdims=True)
         for h in range(N_HEADS)], axis=0)
    m_t = jnp.maximum(g_in, b + c_max)
    m_new = jnp.maximum(b_tot + m_prev, m_loc)
    wide = lambda col: jnp.broadcast_to(col, (N_HEADS, L))
    rows = dict(u=b - m_t, inter=jnp.exp(g_in - m_t), e_neg=jnp.exp(-m_t), w=w,
                s_prev=wide(jnp.exp(b_tot + m_prev - m_new)), s_loc=wide(jnp.exp(m_loc - m_new)))
    return jnp.stack([rows[name] for name in GATE_ROWS]), r_cols, m_new


def _mlstm_heads(d, k_ref, qt_ref, vt_ref, mt_ref, gate_rows, r_cols, late_zero, o_ref, ct_sc, L):
    u, inter, e_neg, w, s_prev, s_loc = (gate_rows[i] for i in range(len(GATE_ROWS)))
    allowed = mt_ref[d] > 0.5
    ones_rows = (lax.broadcasted_iota(jnp.int32, (LANES, L), 0) == 0).astype(BF16)
    for h in range(N_HEADS):
        rows = slice(h * LANES, (h + 1) * LANES)
        hh = slice(h, h + 1)
        qt = qt_ref[rows, :]
        k = k_ref[:, rows]
        v_aug = jnp.concatenate([vt_ref[rows, :], ones_rows], axis=0)
        ct_prev = ct_sc[d, h]
        masked_r = jnp.where(allowed, r_cols[:, h:h + 1], -jnp.inf)
        u_h = u[hh, :] if h < N_HEADS // 2 else u[hh, :] + late_zero[hh, :]
        e_t = jnp.exp(masked_r + u_h)
        p_t = (jnp.dot(k, qt, preferred_element_type=F32) * e_t).astype(BF16)
        nd = (jnp.dot(v_aug, p_t, preferred_element_type=F32)
              + inter[hh, :] * jnp.dot(ct_prev.astype(BF16), qt, preferred_element_type=F32))
        den = jnp.maximum(jnp.abs(nd[LANES:LANES + 1, :]), e_neg[hh, :])
        o_ref[rows, :] = nd[:LANES, :] / den
        wv = (v_aug.astype(F32) * w[hh, :]).astype(BF16)
        ct_loc = jnp.dot(wv, k, preferred_element_type=F32)
        ct_sc[d, h] = s_prev[hh, 0:1] * ct_prev + s_loc[hh, 0:1] * ct_loc


def _mlstm(k, qt, vt, gt, bias, b, s):
    L = min(ML_CHUNK, s)
    nc = s // L
    tri = jnp.tril(jnp.ones((L, L), F32))
    masks_t = jnp.stack([tri.T, tri])
    bias_b = jnp.broadcast_to(bias.reshape(N_GATES, 1).astype(F32), (N_GATES, L))
    fwd = lambda bi, c: bi * nc + c
    bwd = lambda bi, c: bi * nc + nc - 1 - c
    fwd_next = lambda bi, c: bi * nc + jnp.minimum(c + 1, nc - 1)
    bwd_next = lambda bi, c: bi * nc + jnp.maximum(nc - 2 - c, 0)

    def seq_specs(pos, pos_next):
        fm = lambda n, p: pl.BlockSpec((n, L), lambda bi, c: (0, p(bi, c)))
        return [pl.BlockSpec((L, D_MODEL), lambda bi, c: (pos(bi, c), 0)), fm(D_MODEL, pos), fm(D_MODEL, pos),
                fm(N_GATES, pos), fm(N_GATES, pos_next)]

    out_spec = lambda pos: pl.BlockSpec((D_MODEL, L), lambda bi, c: (0, pos(bi, c)))
    out = jax.ShapeDtypeStruct((D_MODEL, b * s), F32)
    return pl.pallas_call(
        functools.partial(_mlstm_kernel, L=L),
        grid=(b, nc),
        in_specs=(seq_specs(fwd, fwd_next) + seq_specs(bwd, bwd_next)
                  + [pl.BlockSpec((N_GATES, L), lambda bi, c: (0, 0)),
                     pl.BlockSpec((2, L, L), lambda bi, c: (0, 0, 0))]),
        out_specs=[out_spec(fwd), out_spec(bwd)],
        out_shape=[out, out],
        scratch_shapes=[pltpu.VMEM((2, N_HEADS, 2 * LANES, LANES), F32),
                        pltpu.VMEM((2, N_HEADS, LANES), F32),
                        pltpu.VMEM((2, 2, len(GATE_ROWS), N_HEADS, L), F32),
                        pltpu.VMEM((2, 2, L, LANES), F32)],
        compiler_params=_cparams(("parallel", "arbitrary")),
        name="mlstm_scan",
    )(k, qt, vt, gt, gt, k, qt, vt, gt, gt, bias_b, masks_t)


def _mlstm_out_kernel(x_ref, hf_ref, hb_ref, ot_ref, hg_ref, w_ref, out_ref):
    tm = x_ref.shape[0]
    hs = hf_ref[...] + hb_ref[...]
    parts = []
    for h in range(N_HEADS):
        blk = hs[h * LANES:(h + 1) * LANES, :]
        ms = jnp.mean(blk * blk, axis=0, keepdims=True)
        parts.append(blk * lax.rsqrt(ms + EPS))
    gain = jnp.concatenate([hg_ref[...]] * (tm // LANES), axis=1)
    hn_t = (jnp.concatenate(parts, axis=0) * gain * jax.nn.sigmoid(ot_ref[...])).astype(BF16)
    out_ref[...] = x_ref[...] + lax.dot_general(hn_t, w_ref[...], (((0,), (0,)), ((), ())),
                                                preferred_element_type=F32)


def _mlstm_out(x2, hf, hb, ot, head_g, w_out, tm):
    t = x2.shape[0]
    gain = jnp.broadcast_to(head_g.reshape(D_MODEL, 1).astype(F32), (D_MODEL, LANES))
    return pl.pallas_call(
        _mlstm_out_kernel,
        grid=(t // tm,),
        in_specs=[pl.BlockSpec((tm, D_MODEL), lambda i: (i, 0)),
                  pl.BlockSpec((D_MODEL, tm), lambda i: (0, i)),
                  pl.BlockSpec((D_MODEL, tm), lambda i: (0, i)),
                  pl.BlockSpec((D_MODEL, tm), lambda i: (0, i)),
                  pl.BlockSpec((D_MODEL, LANES), lambda i: (0, 0)),
                  pl.BlockSpec((D_MODEL, D_MODEL), lambda i: (0, 0))],
        out_specs=pl.BlockSpec((tm, D_MODEL), lambda i: (i, 0)),
        out_shape=jax.ShapeDtypeStruct((t, D_MODEL), F32),
        compiler_params=_cparams(("parallel",)),
        name="mlstm_out",
    )(x2, hf, hb, ot, gain, w_out)


def _mlstm_layout(w_in):
    nq = N_HEADS * ML_DQK
    pad_heads = lambda w: jnp.pad(w.reshape(D_MODEL, N_HEADS, ML_DQK),
                                  ((0, 0), (0, 0), (0, LANES - ML_DQK))).reshape(D_MODEL, N_HEADS * LANES)
    w_q = pad_heads(w_in[:, :nq])
    w_k = pad_heads(w_in[:, nq:2 * nq])
    rest = w_in[:, 2 * nq:]
    w_rows = jnp.concatenate([w_q, rest], axis=1).T
    return w_k.astype(BF16), w_rows.astype(BF16)


def _da_proj_kernel(x_ref, g_ref, w_ref, pos_ref, inv_ref, wts_ref, gq_ref, gk_ref,
                    cq_ref, ck_ref, q_ref, k_ref, v_ref):
    hn = _rms_rows(x_ref[...], g_ref[...]).astype(BF16)
    ang = pos_ref[...] * inv_ref[...]
    cos = jnp.cos(ang)
    sin = jnp.sin(ang)
    trig_q = cq_ref[0:1, :] * cos + cq_ref[1:2, :] * sin + cq_ref[2:3, :]
    trig_k = ck_ref[0:1, :] * cos + ck_ref[1:2, :] * sin + ck_ref[2:3, :]
    fq = trig_q * gq_ref[...]
    fk = trig_k * gk_ref[...]
    nsub = 2 * N_HEADS
    per_dot = MXU_LANES // LANES
    for u0 in range(0, nsub, per_dot):
        for off, fac, wts, o_ref in ((0, fq, wts_ref[0:1, :], q_ref), (nsub * LANES, fk, wts_ref[1:2, :], k_ref)):
            raw2 = jnp.dot(hn, w_ref[:, off + u0 * LANES: off + (u0 + per_dot) * LANES],
                           preferred_element_type=F32)
            for du in range(per_dot):
                raw = raw2[:, du * LANES:(du + 1) * LANES]
                ms = jnp.sum(raw * raw * wts, axis=-1, keepdims=True) * (1.0 / DA_DHEAD)
                o_ref[:, (u0 + du) * LANES:(u0 + du + 1) * LANES] = (raw * lax.rsqrt(ms + EPS) * fac).astype(BF16)
    v_ref[...] = jnp.dot(hn, w_ref[:, 2 * nsub * LANES:], preferred_element_type=F32).astype(BF16)


def _da_layout(w_qkv, q_g, k_g):
    x1 = list(range(0, 8))
    x2 = list(range(8, 16))
    rest = list(range(16, 64))
    q_groups = [(x1, 1, 0, 0), (x1, 1, 0, 0), (x2, 0, -1, 0), (x2, 0, -1, 0),
                (x2, 1, 0, 0), (x2, 1, 0, 0), (x1, 0, 1, 0), (x1, 0, 1, 0)]
    k_groups = [(x1, 1, 0, 0), (x2, 0, -1, 0), (x1, 1, 0, 0), (x2, 0, -1, 0),
                (x2, 1, 0, 0), (x1, 0, 1, 0), (x2, 1, 0, 0), (x1, 0, 1, 0)]

    def tables(groups):
        src, coef = [], []
        for dims, a, b_, c_ in groups:
            src += dims
            coef += [(a, b_, c_)] * len(dims)
        src += rest
        coef += [(0, 0, 1)] * len(rest)
        pad = LANES - len(src)
        valid = [1.0] * len(src) + [0.0] * pad
        src += [0] * pad
        coef += [(0, 0, 0)] * pad
        return jnp.array(src, jnp.int32), jnp.array(coef, F32).T, jnp.array(valid, F32)

    src_q, coef_q, valid_q = tables(q_groups)
    src_k, coef_k, valid_k = tables(k_groups)
    nsub = 2 * N_HEADS
    wq = w_qkv[:, :nsub * DA_DHEAD].reshape(D_MODEL, nsub, DA_DHEAD)
    wk = w_qkv[:, nsub * DA_DHEAD:2 * nsub * DA_DHEAD].reshape(D_MODEL, nsub, DA_DHEAD)
    wv = w_qkv[:, 2 * nsub * DA_DHEAD:]
    wq_p = (wq[:, :, src_q] * valid_q).reshape(D_MODEL, nsub * LANES)
    wk_p = (wk[:, :, src_k] * valid_k).reshape(D_MODEL, nsub * LANES)
    w_cat = jnp.concatenate([wq_p, wk_p, wv], axis=1).astype(BF16)
    lanes = jnp.arange(LANES)
    plain = (lanes >= 64) & (lanes < 112)
    wts_q = ((lanes < 8) | ((lanes >= 16) & (lanes < 24)) | plain).astype(F32)
    wts_k = ((lanes < 16) | plain).astype(F32)
    wts = jnp.concatenate([wts_q[None], wts_k[None], jnp.zeros((6, LANES), F32)], axis=0)
    gq = (q_g[src_q] * valid_q).reshape(1, LANES) * (DA_DHEAD ** -0.5 * math.log2(math.e))
    gk = (k_g[src_k] * valid_k).reshape(1, LANES)
    freq = jnp.where(lanes < 64, lanes % ROPE_HALF, 0)
    inv = ROPE_THETA ** (-freq.astype(F32) * 2.0 / (2 * ROPE_HALF))
    inv = jnp.where(lanes < 64, inv, 0.0).reshape(1, LANES)
    pad8 = lambda a: jnp.concatenate([a, jnp.zeros((5, LANES), F32)], axis=0)
    return w_cat, inv, wts, gq, gk, pad8(coef_q), pad8(coef_k)


def _da_proj(x2, norm_g, w_qkv, q_g, k_g, pos, tm):
    t = x2.shape[0]
    w_cat, inv, wts, gq, gk, cq, ck = _da_layout(w_qkv, q_g, k_g)
    nq = 2 * N_HEADS * LANES
    const = lambda shp: pl.BlockSpec(shp, lambda i: (0,) * len(shp))
    return pl.pallas_call(
        _da_proj_kernel,
        grid=(t // tm,),
        in_specs=[pl.BlockSpec((tm, D_MODEL), lambda i: (i, 0)),
                  const((1, D_MODEL)), const(w_cat.shape),
                  pl.BlockSpec((tm, 1), lambda i: (i, 0)),
                  const((1, LANES)), const((8, LANES)), const((1, LANES)), const((1, LANES)),
                  const((8, LANES)), const((8, LANES))],
        out_specs=[pl.BlockSpec((tm, nq), lambda i: (i, 0)),
                   pl.BlockSpec((tm, nq), lambda i: (i, 0)),
                   pl.BlockSpec((tm, D_MODEL), lambda i: (i, 0))],
        out_shape=[jax.ShapeDtypeStruct((t, nq), BF16),
                   jax.ShapeDtypeStruct((t, nq), BF16),
                   jax.ShapeDtypeStruct((t, D_MODEL), BF16)],
        compiler_params=_cparams(("parallel",)),
        name="da_proj",
    )(x2, norm_g.reshape(1, D_MODEL), w_cat, pos, inv, wts, gq, gk, cq, ck)


def _da_attn_kernel(q_ref, k_ref, v_ref, lam_ref, sg_ref, o_ref, s_sc, *, lambda_init, kb):
    out_scale = 1.0 - lambda_init
    lam_v = lam_ref[...]
    lam = (jnp.exp(jnp.sum(lam_v[0:1] * lam_v[1:2], axis=-1, keepdims=True))
           - jnp.exp(jnp.sum(lam_v[2:3] * lam_v[3:4], axis=-1, keepdims=True))
           + lambda_init)
    tq = q_ref.shape[0]
    th = tq // 2
    n_blk = k_ref.shape[0] // kb
    subs = (0, 1)

    ones_blk = (lax.broadcasted_iota(jnp.int32, (kb, LANES), 1) == 0).astype(BF16)

    def score_block(qs, r0, blk, m_part):
        rows = slice(blk * kb, (blk + 1) * kb)
        for c in subs:
            s = lax.dot_general(qs[c], k_ref[rows, c * LANES:(c + 1) * LANES], (((1,), (1,)), ((), ())),
                                preferred_element_type=F32)
            s_sc[c, r0:r0 + th, rows] = s
            for j in range(kb // LANES):
                m_part[c] = jnp.maximum(m_part[c], s[:, j * LANES:(j + 1) * LANES])

    def attend_block(m, r0, blk, acc):
        rows = slice(blk * kb, (blk + 1) * kb)
        v_aug = jnp.concatenate([v_ref[rows, :], ones_blk], axis=1)
        for c in subs:
            p = jnp.exp2(s_sc[c, r0:r0 + th, rows] - m[c])
            acc[c] = acc[c] + jnp.dot(p.astype(BF16), v_aug, preferred_element_type=F32)

    def finish(r0, acc):
        outs = [acc[c][:, :LANES] / acc[c][:, LANES:LANES + 1] for c in subs]
        o = outs[0] - lam * outs[1]
        ms = jnp.mean(o * o, axis=-1, keepdims=True)
        o_ref[r0:r0 + th, :] = ((o * lax.rsqrt(ms + EPS) * sg_ref[...]) * out_scale).astype(o_ref.dtype)

    new_max = lambda: [jnp.full((th, LANES), -jnp.inf, F32) for _ in subs]
    new_acc = lambda: [jnp.zeros((th, 2 * LANES), F32) for _ in subs]
    row_max = lambda m_part: [jnp.max(m_part[c], axis=-1, keepdims=True) for c in subs]

    q_a = [q_ref[:th, c * LANES:(c + 1) * LANES] for c in subs]
    mp_a, acc_a = new_max(), new_acc()
    for blk in range(n_blk):
        score_block(q_a, 0, blk, mp_a)
    m_a = row_max(mp_a)
    bits = lax.bitcast_convert_type(m_a[0] + m_a[1], jnp.uint32)
    zero = ((bits >> 16) >> 16).astype(F32)
    q_b = [(q_ref[th:, c * LANES:(c + 1) * LANES].astype(F32) + zero).astype(BF16) for c in subs]
    mp_b, acc_b = new_max(), new_acc()
    for blk in range(n_blk):
        score_block(q_b, th, blk, mp_b)
        attend_block(m_a, 0, blk, acc_a)
    finish(0, acc_a)
    m_b = row_max(mp_b)
    for blk in range(n_blk):
        attend_block(m_b, th, blk, acc_b)
    finish(th, acc_b)


DA_KEY_BLOCK = 1024


def _da_attn(q, k, v, lam4, subln_g, b, s, lambda_init, tq):
    tq = min(tq, s)
    kb = min(DA_KEY_BLOCK, s)
    return pl.pallas_call(
        functools.partial(_da_attn_kernel, lambda_init=lambda_init, kb=kb),
        grid=(b, N_HEADS, s // tq),
        in_specs=[pl.BlockSpec((None, tq, 2 * LANES), lambda bi, h, i: (bi, i, h)),
                  pl.BlockSpec((None, s, 2 * LANES), lambda bi, h, i: (bi, 0, h)),
                  pl.BlockSpec((None, s, LANES), lambda bi, h, i: (bi, 0, h)),
                  pl.BlockSpec((4, DA_DHEAD), lambda bi, h, i: (0, 0)),
                  pl.BlockSpec((1, LANES), lambda bi, h, i: (0, 0))],
        out_specs=pl.BlockSpec((None, tq, LANES), lambda bi, h, i: (bi, i, h)),
        out_shape=jax.ShapeDtypeStruct((b, s, D_MODEL), BF16),
        scratch_shapes=[pltpu.VMEM((2, tq, s), F32)],
        compiler_params=_cparams(("parallel", "parallel", "arbitrary")),
        name="da_attn",
    )(q, k, v, lam4, subln_g.reshape(1, LANES))


def _resid_matmul_kernel(x_ref, a_ref, w_ref, out_ref):
    out_ref[...] = x_ref[...] + jnp.dot(a_ref[...], w_ref[...], preferred_element_type=F32)


def _resid_matmul(x2, a2, w, tm):
    t = x2.shape[0]
    return pl.pallas_call(
        _resid_matmul_kernel,
        grid=(t // tm,),
        in_specs=[pl.BlockSpec((tm, D_MODEL), lambda i: (i, 0)),
                  pl.BlockSpec((tm, D_MODEL), lambda i: (i, 0)),
                  pl.BlockSpec((D_MODEL, D_MODEL), lambda i: (0, 0))],
        out_specs=pl.BlockSpec((tm, D_MODEL), lambda i: (i, 0)),
        out_shape=jax.ShapeDtypeStruct((t, D_MODEL), F32),
        compiler_params=_cparams(("parallel",)),
        name="resid_matmul",
    )(x2, a2, w)


N_EXTRACT = PEER_TOPK + 1
A_ROWS = 24
N_FULL_ROWS = 8


RANK_NONE = 64.0


def _oddeven_merge_sort_pairs(n):
    pairs = []

    def merge(lo, m, r):
        step = 2 * r
        if step < m:
            merge(lo, m, step)
            merge(lo + r, m, step)
            pairs.extend((i, i + r) for i in range(lo + r, lo + m - r, step))
        else:
            pairs.append((lo, lo + r))

    def sort(lo, m):
        if m > 1:
            sort(lo, m // 2)
            sort(lo + m // 2, m // 2)
            merge(lo, m, 1)

    sort(0, n)
    return pairs


def _extract_desc(vals, a_sc):
    n_grp = vals.shape[0] // SUBLANES
    network = [(i, j) for i, j in _oddeven_merge_sort_pairs(1 << (n_grp - 1).bit_length()) if j < n_grp]
    for lt in range(vals.shape[1] // LANES):
        ls = slice(lt * LANES, (lt + 1) * LANES)
        col = [vals[v * SUBLANES:(v + 1) * SUBLANES, ls] for v in range(n_grp)]
        for i, j in network:
            col[i], col[j] = jnp.maximum(col[i], col[j]), jnp.minimum(col[i], col[j])
        for t in range(N_EXTRACT):
            mx = jnp.max(col[0], axis=0, keepdims=True)
            a_sc[t:t + 1, ls] = mx
            remaining = N_EXTRACT - 1 - t
            if remaining:
                hit = col[0] == mx
                for v in range(min(remaining, n_grp)):
                    below = col[v + 1] if v + 1 < n_grp else -jnp.inf
                    col[v] = jnp.where(hit, below, col[v])


def _ranks_of(vals, a):
    rank = jnp.full(vals.shape, RANK_NONE, F32)
    for qq in range(N_EXTRACT):
        rank = jnp.where(vals == a[qq:qq + 1, :], float(qq), rank)
    return rank


def _dup_bf16_words(x):
    hi = lax.bitcast_convert_type(x.astype(BF16).astype(F32), jnp.uint32)
    return hi | (hi >> 16)


def _peer_route_kernel(x_ref, g_ref, wq_ref, keys_ref, hnt_ref, cnt_ref, e1z_ref, r2_ref, e2_ref,
                       a1_sc, a2_sc, c_sc):
    tm = x_ref.shape[0]
    hn = _rms_rows(x_ref[...], g_ref[...])
    hnt_ref[...] = hn.T.astype(BF16)
    q = jnp.dot(hn.astype(BF16), wq_ref[...], preferred_element_type=F32).astype(BF16)
    neg = jnp.full((A_ROWS, tm), -jnp.inf, F32)
    for h in range(N_HEADS):
        s_t = []
        for p, a_sc in ((0, a1_sc), (1, a2_sc)):
            hp = 2 * h + p
            st = lax.dot_general(keys_ref[hp], q[:, hp * LANES:(hp + 1) * LANES],
                                 (((1,), (1,)), ((), ())), preferred_element_type=F32)
            a_sc[...] = neg
            _extract_desc(st, a_sc)
            s_t.append(st)
        a1 = a1_sc[...]
        a2 = a2_sc[...]
        rank2 = _ranks_of(s_t[1], a2)
        blocks = [a1[0:1, :] + a2]
        blocks += [a1[p:p + 1, :] + a2[0:SUBLANES, :] for p in range(1, N_FULL_ROWS)]
        blocks.append(a1[N_FULL_ROWS:, :] + a2[0:1, :])
        c_sc[...] = neg
        _extract_desc(jnp.concatenate(blocks, axis=0), c_sc)
        cv = c_sc[...]
        topk_rows = lax.broadcasted_iota(jnp.int32, (A_ROWS, tm), 0) < PEER_TOPK
        z = jnp.sum(jnp.where(topk_rows, jnp.exp(cv - cv[0:1, :]), 0.0), axis=0, keepdims=True)
        tau = 0.5 * (cv[PEER_TOPK - 1:PEER_TOPK, :] + cv[PEER_TOPK:PEER_TOPK + 1, :])
        s1, s2 = s_t
        thr_rank = tau - a1
        cnt_rank = jnp.zeros_like(a1)
        for qq in range(N_EXTRACT):
            cnt_rank = cnt_rank + jnp.where(a2[qq:qq + 1, :] >= thr_rank, 1.0, 0.0)
        cnt = jnp.zeros_like(s1)
        for pp in range(N_EXTRACT):
            cnt = jnp.where(s1 == a1[pp:pp + 1, :], cnt_rank[pp:pp + 1, :], cnt)
        cnt_ref[h] = _dup_bf16_words(cnt)
        e1z_ref[h] = _dup_bf16_words(jnp.exp(s1 - a1[0:1, :]) * (0.5 / z))
        r2_ref[h] = pltpu.bitcast(rank2.astype(BF16), jnp.uint32)
        e2_ref[h] = pltpu.bitcast(jnp.exp(s2 - a2[0:1, :]).astype(BF16), jnp.uint32)


def _peer_route(x2, norm_g, wq, keys, tm):
    t = x2.shape[0]
    row_spec = pl.BlockSpec((N_HEADS, N_KEYS, tm), lambda i: (0, 0, i))
    pair_spec = pl.BlockSpec((N_HEADS, N_KEYS // 2, tm), lambda i: (0, 0, i))
    row_tab = jax.ShapeDtypeStruct((N_HEADS, N_KEYS, t), jnp.uint32)
    pair_tab = jax.ShapeDtypeStruct((N_HEADS, N_KEYS // 2, t), jnp.uint32)
    return pl.pallas_call(
        _peer_route_kernel,
        grid=(t // tm,),
        in_specs=[pl.BlockSpec((tm, D_MODEL), lambda i: (i, 0)),
                  pl.BlockSpec((1, D_MODEL), lambda i: (0, 0)),
                  pl.BlockSpec(wq.shape, lambda i: (0, 0)),
                  pl.BlockSpec(keys.shape, lambda i: (0, 0, 0))],
        out_specs=[pl.BlockSpec((D_MODEL, tm), lambda i: (0, i)), row_spec, row_spec, pair_spec, pair_spec],
        out_shape=[jax.ShapeDtypeStruct((D_MODEL, t), BF16), row_tab, row_tab, pair_tab, pair_tab],
        scratch_shapes=[pltpu.VMEM((A_ROWS, tm), F32) for _ in range(3)],
        compiler_params=_cparams(("parallel",)),
        name="peer_route",
    )(x2, norm_g.reshape(1, D_MODEL), wq, keys)


PEER_EC = SUBLANES * N_KEYS
PIPE_STAGES = 3
I_GROUP = 2


def _peer_expert_kernel(hnt_ref, u_ref, vt_ref, cnt_ref, e1z_ref, r2_ref, e2_ref, x_ref, out_ref,
                        acc_sc, s0_sc, s1_sc, p0_sc, p1_sc, *, ec):
    g = pl.program_id(1)
    tm = hnt_ref.shape[1]
    n_i = ec // N_KEYS

    def row_bf16(rows, k, ls):
        words = jnp.broadcast_to(rows[k:k + 1, ls], (N_KEYS // 2, LANES))
        return pltpu.bitcast(words, BF16)

    zero = jnp.zeros((), BF16)

    def gate_tile(ci, lt, s_in, p_out):
        ls = slice(lt * LANES, (lt + 1) * LANES)
        for i0 in range(0, n_i, I_GROUP):
            gates = [jnp.zeros((N_KEYS, LANES), BF16) for _ in range(I_GROUP)]
            for h in range(N_HEADS):
                r2t = pltpu.bitcast(r2_ref[h][:, ls], BF16)
                e2t = pltpu.bitcast(e2_ref[h][:, ls], BF16)
                tile0 = pl.multiple_of(ci * n_i + (i0 // SUBLANES) * SUBLANES, SUBLANES)
                cnt8 = cnt_ref[h, pl.ds(tile0, SUBLANES), :]
                e1z8 = e1z_ref[h, pl.ds(tile0, SUBLANES), :]
                for kk in range(I_GROUP):
                    k = i0 % SUBLANES + kk
                    sel = jnp.where(r2t < row_bf16(cnt8, k, ls), e2t, zero)
                    gates[kk] = gates[kk] + sel * row_bf16(e1z8, k, ls)
            for kk in range(I_GROUP):
                r0 = (i0 + kk) * N_KEYS
                s = s_in[r0:r0 + N_KEYS, ls].astype(BF16)
                act2 = s * (1.0 + lax.erf(s * (2.0 ** -0.5)))
                p_out[r0 // 2:(r0 + N_KEYS) // 2, ls] = pltpu.bitcast(gates[kk] * act2, jnp.uint32)

    def pipeline_step(c, u_blk, vt_blk, s_in, s_out, p_in, p_out, run_out, run_gate, run_pre):
        tiles_per_slab = MXU_LANES // LANES
        for sb in range(tm // MXU_LANES):
            cs = slice(sb * MXU_LANES, (sb + 1) * MXU_LANES)
            if run_out:
                acc_sc[:, cs] += jnp.dot(vt_blk, pltpu.bitcast(p_in[:, cs], BF16),
                                         preferred_element_type=F32)
            if run_gate:
                gate_tile(c - 1, sb * tiles_per_slab, s_in, p_out)
            if run_pre:
                s_out[:, cs] = jnp.dot(u_blk, hnt_ref[:, cs], preferred_element_type=F32)
            if run_gate:
                for lt in range(sb * tiles_per_slab + 1, (sb + 1) * tiles_per_slab):
                    gate_tile(c - 1, lt, s_in, p_out)

    def grid_step(first, last):
        even = (u_ref[:ec, :], vt_ref[:, :ec], s1_sc, s0_sc, p1_sc, p0_sc)
        odd = (u_ref[ec:, :], vt_ref[:, ec:], s0_sc, s1_sc, p0_sc, p1_sc)
        pipeline_step(2 * g, *even, run_out=not first, run_gate=not first, run_pre=not last)
        pipeline_step(2 * g + 1, *odd, run_out=not first, run_gate=not last, run_pre=not last)

    last_g = pl.num_programs(1) - 1

    @pl.when(g == 0)
    def _():
        acc_sc[...] = jnp.zeros_like(acc_sc)
        grid_step(True, False)

    @pl.when((g > 0) & (g < last_g))
    def _():
        grid_step(False, False)

    @pl.when(g == last_g)
    def _():
        grid_step(False, True)
        out_ref[...] = x_ref[...] + acc_sc[...].T


def _peer_expert(x2, hnt, u, vt, cnt, e1z, r2, e2, tm, ec):
    t = x2.shape[0]
    assert ec % (SUBLANES * N_KEYS) == 0 and (N_EXPERTS // ec) % 2 == 0
    n_chunks = N_EXPERTS // ec
    n_steps = (n_chunks + PIPE_STAGES - 1) // 2
    last_u = n_chunks // 2 - 1
    row_spec = pl.BlockSpec((N_HEADS, N_KEYS, tm), lambda i, g: (0, 0, i))
    pair_spec = pl.BlockSpec((N_HEADS, N_KEYS // 2, tm), lambda i, g: (0, 0, i))
    return pl.pallas_call(
        functools.partial(_peer_expert_kernel, ec=ec),
        grid=(t // tm, n_steps),
        in_specs=[pl.BlockSpec((D_MODEL, tm), lambda i, g: (0, i)),
                  pl.BlockSpec((2 * ec, D_MODEL), lambda i, g: (jnp.minimum(g, last_u), 0)),
                  pl.BlockSpec((D_MODEL, 2 * ec), lambda i, g: (0, jnp.maximum(g - 1, 0))),
                  row_spec, row_spec, pair_spec, pair_spec,
                  pl.BlockSpec((tm, D_MODEL), lambda i, g: (i, 0))],
        out_specs=pl.BlockSpec((tm, D_MODEL), lambda i, g: (i, 0)),
        out_shape=jax.ShapeDtypeStruct((t, D_MODEL), F32),
        scratch_shapes=[pltpu.VMEM((D_MODEL, tm), F32),
                        pltpu.VMEM((ec, tm), F32), pltpu.VMEM((ec, tm), F32),
                        pltpu.VMEM((ec // 2, tm), jnp.uint32), pltpu.VMEM((ec // 2, tm), jnp.uint32)],
        compiler_params=_cparams(("parallel", "arbitrary")),
        name="peer_expert",
    )(hnt, u, vt, cnt, e1z, r2, e2, x2)


def _peer_ffn(x2, norm_g, w_query, sub_keys, u_table, v_table, tm_route, tm_exp, ec):
    keys = sub_keys.reshape(2 * N_HEADS, N_KEYS, LANES).astype(BF16)
    hnt, cnt, e1z, r2, e2 = _peer_route(x2, norm_g, w_query.astype(BF16), keys, tm_route)
    return _peer_expert(x2, hnt, u_table.astype(BF16), v_table.T.astype(BF16), cnt, e1z, r2, e2, tm_exp, ec)


def kernel(x, positions, l0_norm_mix, l0_w_in, l0_b_gates, l0_head_norm, l0_w_out, l0_norm_ffn, l0_peer_wq, l0_peer_keys, l0_peer_u, l0_peer_v, l1_norm_mix, l1_w_qkv, l1_q_norm, l1_k_norm, l1_lambda_q1, l1_lambda_k1, l1_lambda_q2, l1_lambda_k2, l1_subln, l1_w_out, l1_norm_ffn, l1_peer_wq, l1_peer_keys, l1_peer_u, l1_peer_v):
    b, s, d = x.shape
    t = b * s
    tm = min(TOKEN_TILE, t)
    tm_route = min(ROUTE_TILE, t)
    x2 = x.reshape(t, d)

    w_k, w_rows = _mlstm_layout(l0_w_in)
    k, qt, vt, ot, gt = _mlstm_proj(x2, l0_norm_mix, w_k, w_rows, tm)
    hf, hb = _mlstm(k, qt, vt, gt, l0_b_gates, b, s)
    x2 = _mlstm_out(x2, hf, hb, ot, l0_head_norm, l0_w_out.astype(BF16), tm)
    x2 = _peer_ffn(x2, l0_norm_ffn, l0_peer_wq, l0_peer_keys, l0_peer_u, l0_peer_v, tm_route, tm, PEER_EC)

    lambda_init = 0.8 - 0.6 * math.exp(-0.3 * 1)
    pos = positions.reshape(t, 1).astype(F32)
    qa, ka, va = _da_proj(x2, l1_norm_mix, l1_w_qkv, l1_q_norm, l1_k_norm, pos, tm)
    to3 = lambda a: a.reshape(b, s, a.shape[-1])
    lam4 = jnp.stack([l1_lambda_q1, l1_lambda_k1, l1_lambda_q2, l1_lambda_k2]).astype(F32)
    att = _da_attn(to3(qa), to3(ka), to3(va), lam4, l1_subln, b, s, lambda_init, DA_QUERY_TILE)
    x2 = _resid_matmul(x2, att.reshape(t, D_MODEL), l1_w_out.astype(BF16), tm)
    x2 = _peer_ffn(x2, l1_norm_ffn, l1_peer_wq, l1_peer_keys, l1_peer_u, l1_peer_v, tm_route, tm, PEER_EC)
    return x2.reshape(b, s, d)
```

```python
import functools
import math

import jax
import jax.numpy as jnp
from jax import lax
from jax.experimental import pallas as pl
from jax.experimental.pallas import tpu as pltpu

F32 = jnp.float32
BF16 = jnp.bfloat16
HIGHEST = lax.Precision.HIGHEST

D_MODEL = 1024
EPS = 1e-6
LANES = 128
SUBLANES = 8
MXU_LANES = 256
N_HEADS = 8
ML_DQK = 64
ML_CHUNK = 256
ML_M_INIT = -1e30
DA_DHEAD = 64
ROPE_THETA = 500000.0
ROPE_HALF = 8
N_KEYS = 128
N_EXPERTS = N_KEYS * N_KEYS
PEER_TOPK = 16
VMEM_LIMIT_BYTES = 56 * 1024 * 1024
TOKEN_TILE = 512
ROUTE_TILE = 256
DA_QUERY_TILE = 512


def _cparams(sem):
    return pltpu.CompilerParams(dimension_semantics=sem, vmem_limit_bytes=VMEM_LIMIT_BYTES)


def _rms_rows(x, g):
    ms = jnp.mean(x * x, axis=-1, keepdims=True)
    return x * lax.rsqrt(ms + EPS) * g


N_GATES = 4 * N_HEADS


def _mlstm_proj_kernel(x_ref, g_ref, wk_ref, wt_ref, k_ref, qt_ref, vt_ref, ot_ref, gt_ref):
    hn = _rms_rows(x_ref[...], g_ref[...]).astype(BF16)
    k_ref[...] = jnp.dot(hn, wk_ref[...], preferred_element_type=F32).astype(BF16)

    def feature_major(r0, n):
        return lax.dot_general(wt_ref[r0:r0 + n, :], hn, (((1,), (1,)), ((), ())), preferred_element_type=F32)

    qt_ref[...] = (feature_major(0, D_MODEL) * (ML_DQK ** -0.5)).astype(BF16)
    vt_ref[...] = feature_major(D_MODEL, D_MODEL).astype(BF16)
    ot_ref[...] = feature_major(2 * D_MODEL, D_MODEL)
    gt_ref[...] = feature_major(3 * D_MODEL, N_GATES)


def _mlstm_proj(x2, g, w_k, w_rows, tm):
    t = x2.shape[0]
    fm = lambda n, dt: (pl.BlockSpec((n, tm), lambda i: (0, i)), jax.ShapeDtypeStruct((n, t), dt))
    outs = [(pl.BlockSpec((tm, D_MODEL), lambda i: (i, 0)), jax.ShapeDtypeStruct((t, D_MODEL), BF16)),
            fm(D_MODEL, BF16), fm(D_MODEL, BF16), fm(D_MODEL, F32), fm(N_GATES, F32)]
    return pl.pallas_call(
        _mlstm_proj_kernel,
        grid=(t // tm,),
        in_specs=[pl.BlockSpec((tm, D_MODEL), lambda i: (i, 0)),
                  pl.BlockSpec((1, D_MODEL), lambda i: (0, 0)),
                  pl.BlockSpec(w_k.shape, lambda i: (0, 0)),
                  pl.BlockSpec(w_rows.shape, lambda i: (0, 0))],
        out_specs=[o[0] for o in outs],
        out_shape=[o[1] for o in outs],
        compiler_params=_cparams(("parallel",)),
        name="mlstm_proj",
    )(x2, g.reshape(1, D_MODEL), w_k, w_rows)


GATE_ROWS = ("u", "inter", "e_neg", "w", "s_prev", "s_loc")


def _mlstm_kernel(kf_ref, qtf_ref, vtf_ref, gtf_ref, gtf_next_ref, kb_ref, qtb_ref, vtb_ref, gtb_ref,
                  gtb_next_ref, bias_ref, mt_ref, of_ref, ob_ref, ct_sc, m_sc, pre_sc, rc_sc, *, L):
    c = pl.program_id(1)
    slot = lax.rem(c, 2)
    streams = ((0, kf_ref, qtf_ref, vtf_ref, gtf_ref, gtf_next_ref, of_ref),
               (1, kb_ref, qtb_ref, vtb_ref, gtb_ref, gtb_next_ref, ob_ref))

    def gate_math(d, gt_ref):
        return _mlstm_gate_math(d, gt_ref, bias_ref, mt_ref, m_sc[d, :, 0:1], L)

    def hand_over(d, to_slot, rows, r_cols, m_new):
        pre_sc[d, to_slot] = rows
        rc_sc[d, to_slot] = r_cols
        m_sc[d] = jnp.broadcast_to(m_new, (N_HEADS, LANES))

    @pl.when(c == 0)
    def _():
        ct_sc[...] = jnp.zeros_like(ct_sc)
        m_sc[...] = jnp.full_like(m_sc, ML_M_INIT)
        for d, _, _, _, gt_ref, _, _ in streams:
            hand_over(d, 0, *gate_math(d, gt_ref))

    for d, k_ref, qt_ref, vt_ref, _, gt_next_ref, o_ref in streams:
        nxt = gate_math(d, gt_next_ref)
        bits = lax.bitcast_convert_type(nxt[2], jnp.uint32)
        zero = ((bits >> 16) >> 16).astype(F32)
        _mlstm_heads(d, k_ref, qt_ref, vt_ref, mt_ref, pre_sc[d, slot], rc_sc[d, slot], zero, o_ref, ct_sc, L)
        hand_over(d, 1 - slot, *nxt)


def _mlstm_gate_math(d, gt_ref, bias_ref, mt_ref, m_prev, L):
    row0 = d * 2 * N_HEADS
    g = gt_ref[row0:row0 + 2 * N_HEADS, :] + bias_ref[row0:row0 + 2 * N_HEADS, :]
    li = g[:N_HEADS]
    gf = g[N_HEADS:]
    lf = jnp.minimum(gf, 0.0) - jnp.log1p(jnp.exp(-jnp.abs(gf)))
    mask_f = mt_ref[d]
    b = jnp.dot(lf, mask_f, precision=HIGHEST, preferred_element_type=F32)
    b_tot = jnp.min(b, axis=1, keepdims=True)
    r = li - b
    a = b_tot + r
    m_loc = jnp.max(a, axis=1, keepdims=True)
    w = jnp.exp(a - m_loc)
    g_in = b + m_prev
    allowed = mask_f > 0.5
    r_cols = jnp.concatenate([r, jnp.zeros((LANES - N_HEADS, L), F32)], axis=0).T
    c_max = jnp.concatenate(
        [jnp.max(jnp.where(allowed, r_cols[:, h:h + 1], -jnp.inf), axis=0, keepdims=True)
         for h in range(N_HEADS)], axis=0)
    m_t = jnp.maximum(g_in, b + c_max)
    m_new = jnp.maximum(b_tot + m_prev, m_loc)
    wide = lambda col: jnp.broadcast_to(col, (N_HEADS, L))
    rows = dict(u=b - m_t, inter=jnp.exp(g_in - m_t), e_neg=jnp.exp(-m_t), w=w,
                s_prev=wide(jnp.exp(b_tot + m_prev - m_new)), s_loc=wide(jnp.exp(m_loc - m_new)))
    return jnp.stack([rows[name] for name in GATE_ROWS]), r_cols, m_new


def _mlstm_heads(d, k_ref, qt_ref, vt_ref, mt_ref, gate_rows, r_cols, late_zero, o_ref, ct_sc, L):
    u, inter, e_neg, w, s_prev, s_loc = (gate_rows[i] for i in range(len(GATE_ROWS)))
    allowed = mt_ref[d] > 0.5
    ones_rows = (lax.broadcasted_iota(jnp.int32, (LANES, L), 0) == 0).astype(BF16)
    for h in range(N_HEADS):
        rows = slice(h * LANES, (h + 1) * LANES)
        hh = slice(h, h + 1)
        qt = qt_ref[rows, :]
        k = k_ref[:, rows]
        v_aug = jnp.concatenate([vt_ref[rows, :], ones_rows], axis=0)
        ct_prev = ct_sc[d, h]
        masked_r = jnp.where(allowed, r_cols[:, h:h + 1], -jnp.inf)
        u_h = u[hh, :] if h < N_HEADS // 2 else u[hh, :] + late_zero[hh, :]
        e_t = jnp.exp(masked_r + u_h)
        p_t = (jnp.dot(k, qt, preferred_element_type=F32) * e_t).astype(BF16)
        nd = (jnp.dot(v_aug, p_t, preferred_element_type=F32)
              + inter[hh, :] * jnp.dot(ct_prev.astype(BF16), qt, preferred_element_type=F32))
        den = jnp.maximum(jnp.abs(nd[LANES:LANES + 1, :]), e_neg[hh, :])
        o_ref[rows, :] = nd[:LANES, :] / den
        wv = (v_aug.astype(F32) * w[hh, :]).astype(BF16)
        ct_loc = jnp.dot(wv, k, preferred_element_type=F32)
        ct_sc[d, h] = s_prev[hh, 0:1] * ct_prev + s_loc[hh, 0:1] * ct_loc


def _mlstm(k, qt, vt, gt, bias, b, s):
    L = min(ML_CHUNK, s)
    nc = s // L
    tri = jnp.tril(jnp.ones((L, L), F32))
    masks_t = jnp.stack([tri.T, tri])
    bias_b = jnp.broadcast_to(bias.reshape(N_GATES, 1).astype(F32), (N_GATES, L))
    fwd = lambda bi, c: bi * nc + c
    bwd = lambda bi, c: bi * nc + nc - 1 - c
    fwd_next = lambda bi, c: bi * nc + jnp.minimum(c + 1, nc - 1)
    bwd_next = lambda bi, c: bi * nc + jnp.maximum(nc - 2 - c, 0)

    def seq_specs(pos, pos_next):
        fm = lambda n, p: pl.BlockSpec((n, L), lambda bi, c: (0, p(bi, c)))
        return [pl.BlockSpec((L, D_MODEL), lambda bi, c: (pos(bi, c), 0)), fm(D_MODEL, pos), fm(D_MODEL, pos),
                fm(N_GATES, pos), fm(N_GATES, pos_next)]

    out_spec = lambda pos: pl.BlockSpec((D_MODEL, L), lambda bi, c: (0, pos(bi, c)))
    out = jax.ShapeDtypeStruct((D_MODEL, b * s), F32)
    return pl.pallas_call(
        functools.partial(_mlstm_kernel, L=L),
        grid=(b, nc),
        in_specs=(seq_specs(fwd, fwd_next) + seq_specs(bwd, bwd_next)
                  + [pl.BlockSpec((N_GATES, L), lambda bi, c: (0, 0)),
                     pl.BlockSpec((2, L, L), lambda bi, c: (0, 0, 0))]),
        out_specs=[out_spec(fwd), out_spec(bwd)],
        out_shape=[out, out],
        scratch_shapes=[pltpu.VMEM((2, N_HEADS, 2 * LANES, LANES), F32),
                        pltpu.VMEM((2, N_HEADS, LANES), F32),
                        pltpu.VMEM((2, 2, len(GATE_ROWS), N_HEADS, L), F32),
                        pltpu.VMEM((2, 2, L, LANES), F32)],
        compiler_params=_cparams(("parallel", "arbitrary")),
        name="mlstm_scan",
    )(k, qt, vt, gt, gt, k, qt, vt, gt, gt, bias_b, masks_t)


def _mlstm_out_kernel(x_ref, hf_ref, hb_ref, ot_ref, hg_ref, w_ref, out_ref):
    tm = x_ref.shape[0]
    hs = hf_ref[...] + hb_ref[...]
    parts = []
    for h in range(N_HEADS):
        blk = hs[h * LANES:(h + 1) * LANES, :]
        ms = jnp.mean(blk * blk, axis=0, keepdims=True)
        parts.append(blk * lax.rsqrt(ms + EPS))
    gain = jnp.concatenate([hg_ref[...]] * (tm // LANES), axis=1)
    hn_t = (jnp.concatenate(parts, axis=0) * gain * jax.nn.sigmoid(ot_ref[...])).astype(BF16)
    out_ref[...] = x_ref[...] + lax.dot_general(hn_t, w_ref[...], (((0,), (0,)), ((), ())),
                                                preferred_element_type=F32)


def _mlstm_out(x2, hf, hb, ot, head_g, w_out, tm):
    t = x2.shape[0]
    gain = jnp.broadcast_to(head_g.reshape(D_MODEL, 1).astype(F32), (D_MODEL, LANES))
    return pl.pallas_call(
        _mlstm_out_kernel,
        grid=(t // tm,),
        in_specs=[pl.BlockSpec((tm, D_MODEL), lambda i: (i, 0)),
                  pl.BlockSpec((D_MODEL, tm), lambda i: (0, i)),
                  pl.BlockSpec((D_MODEL, tm), lambda i: (0, i)),
                  pl.BlockSpec((D_MODEL, tm), lambda i: (0, i)),
                  pl.BlockSpec((D_MODEL, LANES), lambda i: (0, 0)),
                  pl.BlockSpec((D_MODEL, D_MODEL), lambda i: (0, 0))],
        out_specs=pl.BlockSpec((tm, D_MODEL), lambda i: (i, 0)),
        out_shape=jax.ShapeDtypeStruct((t, D_MODEL), F32),
        compiler_params=_cparams(("parallel",)),
        name="mlstm_out",
    )(x2, hf, hb, ot, gain, w_out)


def _mlstm_layout(w_in):
    nq = N_HEADS * ML_DQK
    pad_heads = lambda w: jnp.pad(w.reshape(D_MODEL, N_HEADS, ML_DQK),
                                  ((0, 0), (0, 0), (0, LANES - ML_DQK))).reshape(D_MODEL, N_HEADS * LANES)
    w_q = pad_heads(w_in[:, :nq])
    w_k = pad_heads(w_in[:, nq:2 * nq])
    rest = w_in[:, 2 * nq:]
    w_rows = jnp.concatenate([w_q, rest], axis=1).T
    return w_k.astype(BF16), w_rows.astype(BF16)


def _da_proj_kernel(x_ref, g_ref, w_ref, pos_ref, inv_ref, wts_ref, gq_ref, gk_ref,
                    cq_ref, ck_ref, q_ref, k_ref, v_ref):
    hn = _rms_rows(x_ref[...], g_ref[...]).astype(BF16)
    ang = pos_ref[...] * inv_ref[...]
    cos = jnp.cos(ang)
    sin = jnp.sin(ang)
    trig_q = cq_ref[0:1, :] * cos + cq_ref[1:2, :] * sin + cq_ref[2:3, :]
    trig_k = ck_ref[0:1, :] * cos + ck_ref[1:2, :] * sin + ck_ref[2:3, :]
    fq = trig_q * gq_ref[...]
    fk = trig_k * gk_ref[...]
    nsub = 2 * N_HEADS
    per_dot = MXU_LANES // LANES
    for u0 in range(0, nsub, per_dot):
        for off, fac, wts, o_ref in ((0, fq, wts_ref[0:1, :], q_ref), (nsub * LANES, fk, wts_ref[1:2, :], k_ref)):
            raw2 = jnp.dot(hn, w_ref[:, off + u0 * LANES: off + (u0 + per_dot) * LANES],
                           preferred_element_type=F32)
            for du in range(per_dot):
                raw = raw2[:, du * LANES:(du + 1) * LANES]
                ms = jnp.sum(raw * raw * wts, axis=-1, keepdims=True) * (1.0 / DA_DHEAD)
                blk = raw * lax.rsqrt(ms + EPS) * fac
                lo, hi = (u0 + du) * LANES, (u0 + du + 1) * LANES
                if o_ref is k_ref:
                    o_ref[lo:hi, :] = blk.T.astype(BF16)
                else:
                    o_ref[:, lo:hi] = blk.astype(BF16)
    v_ref[...] = jnp.dot(hn, w_ref[:, 2 * nsub * LANES:], preferred_element_type=F32).astype(BF16)


def _da_layout(w_qkv, q_g, k_g):
    x1 = list(range(0, 8))
    x2 = list(range(8, 16))
    rest = list(range(16, 64))
    q_groups = [(x1, 1, 0, 0), (x1, 1, 0, 0), (x2, 0, -1, 0), (x2, 0, -1, 0),
                (x2, 1, 0, 0), (x2, 1, 0, 0), (x1, 0, 1, 0), (x1, 0, 1, 0)]
    k_groups = [(x1, 1, 0, 0), (x2, 0, -1, 0), (x1, 1, 0, 0), (x2, 0, -1, 0),
                (x2, 1, 0, 0), (x1, 0, 1, 0), (x2, 1, 0, 0), (x1, 0, 1, 0)]

    def tables(groups):
        src, coef = [], []
        for dims, a, b_, c_ in groups:
            src += dims
            coef += [(a, b_, c_)] * len(dims)
        src += rest
        coef += [(0, 0, 1)] * len(rest)
        pad = LANES - len(src)
        valid = [1.0] * len(src) + [0.0] * pad
        src += [0] * pad
        coef += [(0, 0, 0)] * pad
        return jnp.array(src, jnp.int32), jnp.array(coef, F32).T, jnp.array(valid, F32)

    src_q, coef_q, valid_q = tables(q_groups)
    src_k, coef_k, valid_k = tables(k_groups)
    nsub = 2 * N_HEADS
    wq = w_qkv[:, :nsub * DA_DHEAD].reshape(D_MODEL, nsub, DA_DHEAD)
    wk = w_qkv[:, nsub * DA_DHEAD:2 * nsub * DA_DHEAD].reshape(D_MODEL, nsub, DA_DHEAD)
    wv = w_qkv[:, 2 * nsub * DA_DHEAD:]
    wq_p = (wq[:, :, src_q] * valid_q).reshape(D_MODEL, nsub * LANES)
    wk_p = (wk[:, :, src_k] * valid_k).reshape(D_MODEL, nsub * LANES)
    w_cat = jnp.concatenate([wq_p, wk_p, wv], axis=1).astype(BF16)
    lanes = jnp.arange(LANES)
    plain = (lanes >= 64) & (lanes < 112)
    wts_q = ((lanes < 8) | ((lanes >= 16) & (lanes < 24)) | plain).astype(F32)
    wts_k = ((lanes < 16) | plain).astype(F32)
    wts = jnp.concatenate([wts_q[None], wts_k[None], jnp.zeros((6, LANES), F32)], axis=0)
    gq = (q_g[src_q] * valid_q).reshape(1, LANES) * (DA_DHEAD ** -0.5 * math.log2(math.e))
    gk = (k_g[src_k] * valid_k).reshape(1, LANES)
    freq = jnp.where(lanes < 64, lanes % ROPE_HALF, 0)
    inv = ROPE_THETA ** (-freq.astype(F32) * 2.0 / (2 * ROPE_HALF))
    inv = jnp.where(lanes < 64, inv, 0.0).reshape(1, LANES)
    pad8 = lambda a: jnp.concatenate([a, jnp.zeros((5, LANES), F32)], axis=0)
    return w_cat, inv, wts, gq, gk, pad8(coef_q), pad8(coef_k)


def _da_proj(x2, norm_g, w_qkv, q_g, k_g, pos, tm):
    t = x2.shape[0]
    w_cat, inv, wts, gq, gk, cq, ck = _da_layout(w_qkv, q_g, k_g)
    nq = 2 * N_HEADS * LANES
    const = lambda shp: pl.BlockSpec(shp, lambda i: (0,) * len(shp))
    return pl.pallas_call(
        _da_proj_kernel,
        grid=(t // tm,),
        in_specs=[pl.BlockSpec((tm, D_MODEL), lambda i: (i, 0)),
                  const((1, D_MODEL)), const(w_cat.shape),
                  pl.BlockSpec((tm, 1), lambda i: (i, 0)),
                  const((1, LANES)), const((8, LANES)), const((1, LANES)), const((1, LANES)),
                  const((8, LANES)), const((8, LANES))],
        out_specs=[pl.BlockSpec((tm, nq), lambda i: (i, 0)),
                   pl.BlockSpec((nq, tm), lambda i: (0, i)),
                   pl.BlockSpec((tm, D_MODEL), lambda i: (i, 0))],
        out_shape=[jax.ShapeDtypeStruct((t, nq), BF16),
                   jax.ShapeDtypeStruct((nq, t), BF16),
                   jax.ShapeDtypeStruct((t, D_MODEL), BF16)],
        compiler_params=_cparams(("parallel",)),
        name="da_proj",
    )(x2, norm_g.reshape(1, D_MODEL), w_cat, pos, inv, wts, gq, gk, cq, ck)


def _da_attn_kernel(q_ref, k_ref, v_ref, lam_ref, sg_ref, o_ref, s_sc, *, lambda_init, kb):
    out_scale = 1.0 - lambda_init
    lam_v = lam_ref[...]
    lam = (jnp.exp(jnp.sum(lam_v[0:1] * lam_v[1:2], axis=-1, keepdims=True))
           - jnp.exp(jnp.sum(lam_v[2:3] * lam_v[3:4], axis=-1, keepdims=True))
           + lambda_init)
    tq = q_ref.shape[0]
    th = tq // 2
    n_blk = k_ref.shape[1] // kb
    subs = (0, 1)

    ones_blk = (lax.broadcasted_iota(jnp.int32, (kb, LANES), 1) == 0).astype(BF16)

    def score_block(qs, r0, blk, m_part):
        rows = slice(blk * kb, (blk + 1) * kb)
        for c in subs:
            s = jnp.dot(qs[c], k_ref[c * LANES:(c + 1) * LANES, rows], preferred_element_type=F32)
            s_sc[c, r0:r0 + th, rows] = s
            for j in range(kb // LANES):
                m_part[c] = jnp.maximum(m_part[c], s[:, j * LANES:(j + 1) * LANES])

    def attend_block(m, r0, blk, acc):
        rows = slice(blk * kb, (blk + 1) * kb)
        v_aug = jnp.concatenate([v_ref[rows, :], ones_blk], axis=1)
        for c in subs:
            p = jnp.exp2(s_sc[c, r0:r0 + th, rows] - m[c])
            acc[c] = acc[c] + jnp.dot(p.astype(BF16), v_aug, preferred_element_type=F32)

    def finish(r0, acc):
        outs = [acc[c][:, :LANES] / acc[c][:, LANES:LANES + 1] for c in subs]
        o = outs[0] - lam * outs[1]
        ms = jnp.mean(o * o, axis=-1, keepdims=True)
        o_ref[r0:r0 + th, :] = ((o * lax.rsqrt(ms + EPS) * sg_ref[...]) * out_scale).astype(o_ref.dtype)

    new_max = lambda: [jnp.full((th, LANES), -jnp.inf, F32) for _ in subs]
    new_acc = lambda: [jnp.zeros((th, 2 * LANES), F32) for _ in subs]
    row_max = lambda m_part: [jnp.max(m_part[c], axis=-1, keepdims=True) for c in subs]

    q_a = [q_ref[:th, c * LANES:(c + 1) * LANES] for c in subs]
    mp_a, acc_a = new_max(), new_acc()
    for blk in range(n_blk):
        score_block(q_a, 0, blk, mp_a)
    m_a = row_max(mp_a)
    bits = lax.bitcast_convert_type(m_a[0] + m_a[1], jnp.uint32)
    zero = ((bits >> 16) >> 16).astype(F32)
    q_b = [(q_ref[th:, c * LANES:(c + 1) * LANES].astype(F32) + zero).astype(BF16) for c in subs]
    mp_b, acc_b = new_max(), new_acc()
    for blk in range(n_blk):
        score_block(q_b, th, blk, mp_b)
        attend_block(m_a, 0, blk, acc_a)
    finish(0, acc_a)
    m_b = row_max(mp_b)
    for blk in range(n_blk):
        attend_block(m_b, th, blk, acc_b)
    finish(th, acc_b)


DA_KEY_BLOCK = 1024


def _da_attn(q, k, v, lam4, subln_g, b, s, lambda_init, tq):
    tq = min(tq, s)
    kb = min(DA_KEY_BLOCK, s)
    return pl.pallas_call(
        functools.partial(_da_attn_kernel, lambda_init=lambda_init, kb=kb),
        grid=(b, N_HEADS, s // tq),
        in_specs=[pl.BlockSpec((None, tq, 2 * LANES), lambda bi, h, i: (bi, i, h)),
                  pl.BlockSpec((2 * LANES, s), lambda bi, h, i: (h, bi)),
                  pl.BlockSpec((None, s, LANES), lambda bi, h, i: (bi, 0, h)),
                  pl.BlockSpec((4, DA_DHEAD), lambda bi, h, i: (0, 0)),
                  pl.BlockSpec((1, LANES), lambda bi, h, i: (0, 0))],
        out_specs=pl.BlockSpec((None, tq, LANES), lambda bi, h, i: (bi, i, h)),
        out_shape=jax.ShapeDtypeStruct((b, s, D_MODEL), BF16),
        scratch_shapes=[pltpu.VMEM((2, tq, s), F32)],
        compiler_params=_cparams(("parallel", "parallel", "arbitrary")),
        name="da_attn",
    )(q, k, v, lam4, subln_g.reshape(1, LANES))


def _resid_matmul_kernel(x_ref, a_ref, w_ref, out_ref):
    out_ref[...] = x_ref[...] + jnp.dot(a_ref[...], w_ref[...], preferred_element_type=F32)


def _resid_matmul(x2, a2, w, tm):
    t = x2.shape[0]
    return pl.pallas_call(
        _resid_matmul_kernel,
        grid=(t // tm,),
        in_specs=[pl.BlockSpec((tm, D_MODEL), lambda i: (i, 0)),
                  pl.BlockSpec((tm, D_MODEL), lambda i: (i, 0)),
                  pl.BlockSpec((D_MODEL, D_MODEL), lambda i: (0, 0))],
        out_specs=pl.BlockSpec((tm, D_MODEL), lambda i: (i, 0)),
        out_shape=jax.ShapeDtypeStruct((t, D_MODEL), F32),
        compiler_params=_cparams(("parallel",)),
        name="resid_matmul",
    )(x2, a2, w)


N_EXTRACT = PEER_TOPK + 1
A_ROWS = 24
N_FULL_ROWS = 8


RANK_NONE = 64.0


def _oddeven_merge_sort_pairs(n):
    pairs = []

    def merge(lo, m, r):
        step = 2 * r
        if step < m:
            merge(lo, m, step)
            merge(lo + r, m, step)
            pairs.extend((i, i + r) for i in range(lo + r, lo + m - r, step))
        else:
            pairs.append((lo, lo + r))

    def sort(lo, m):
        if m > 1:
            sort(lo, m // 2)
            sort(lo + m // 2, m // 2)
            merge(lo, m, 1)

    sort(0, n)
    return pairs


def _extract_desc(vals, a_sc):
    n_grp = vals.shape[0] // SUBLANES
    network = [(i, j) for i, j in _oddeven_merge_sort_pairs(1 << (n_grp - 1).bit_length()) if j < n_grp]
    for lt in range(vals.shape[1] // LANES):
        ls = slice(lt * LANES, (lt + 1) * LANES)
        col = [vals[v * SUBLANES:(v + 1) * SUBLANES, ls] for v in range(n_grp)]
        for i, j in network:
            col[i], col[j] = jnp.maximum(col[i], col[j]), jnp.minimum(col[i], col[j])
        for t in range(N_EXTRACT):
            mx = jnp.max(col[0], axis=0, keepdims=True)
            a_sc[t:t + 1, ls] = mx
            remaining = N_EXTRACT - 1 - t
            if remaining:
                hit = col[0] == mx
                for v in range(min(remaining, n_grp)):
                    below = col[v + 1] if v + 1 < n_grp else -jnp.inf
                    col[v] = jnp.where(hit, below, col[v])


def _ranks_of(vals, a):
    rank = jnp.full(vals.shape, RANK_NONE, F32)
    for qq in range(N_EXTRACT):
        rank = jnp.where(vals == a[qq:qq + 1, :], float(qq), rank)
    return rank


def _dup_bf16_words(x):
    hi = lax.bitcast_convert_type(x.astype(BF16).astype(F32), jnp.uint32)
    return hi | (hi >> 16)


def _peer_route_kernel(x_ref, g_ref, wq_ref, keys_ref, hnt_ref, cnt_ref, e1z_ref, r2_ref, e2_ref,
                       a1_sc, a2_sc, c_sc):
    tm = x_ref.shape[0]
    hn = _rms_rows(x_ref[...], g_ref[...])
    hnt_ref[...] = hn.T.astype(BF16)
    q = jnp.dot(hn.astype(BF16), wq_ref[...], preferred_element_type=F32).astype(BF16)
    neg = jnp.full((A_ROWS, tm), -jnp.inf, F32)
    for h in range(N_HEADS):
        s_t = []
        for p, a_sc in ((0, a1_sc), (1, a2_sc)):
            hp = 2 * h + p
            st = lax.dot_general(keys_ref[hp], q[:, hp * LANES:(hp + 1) * LANES],
                                 (((1,), (1,)), ((), ())), preferred_element_type=F32)
            a_sc[...] = neg
            _extract_desc(st, a_sc)
            s_t.append(st)
        a1 = a1_sc[...]
        a2 = a2_sc[...]
        rank2 = _ranks_of(s_t[1], a2)
        blocks = [a1[0:1, :] + a2]
        blocks += [a1[p:p + 1, :] + a2[0:SUBLANES, :] for p in range(1, N_FULL_ROWS)]
        blocks.append(a1[N_FULL_ROWS:, :] + a2[0:1, :])
        c_sc[...] = neg
        _extract_desc(jnp.concatenate(blocks, axis=0), c_sc)
        cv = c_sc[...]
        topk_rows = lax.broadcasted_iota(jnp.int32, (A_ROWS, tm), 0) < PEER_TOPK
        z = jnp.sum(jnp.where(topk_rows, jnp.exp(cv - cv[0:1, :]), 0.0), axis=0, keepdims=True)
        tau = 0.5 * (cv[PEER_TOPK - 1:PEER_TOPK, :] + cv[PEER_TOPK:PEER_TOPK + 1, :])
        s1, s2 = s_t
        thr_rank = tau - a1
        cnt_rank = jnp.zeros_like(a1)
        for qq in range(N_EXTRACT):
            cnt_rank = cnt_rank + jnp.where(a2[qq:qq + 1, :] >= thr_rank, 1.0, 0.0)
        cnt = jnp.zeros_like(s1)
        for pp in range(N_EXTRACT):
            cnt = jnp.where(s1 == a1[pp:pp + 1, :], cnt_rank[pp:pp + 1, :], cnt)
        cnt_ref[h] = _dup_bf16_words(cnt)
        e1z_ref[h] = _dup_bf16_words(jnp.exp(s1 - a1[0:1, :]) * (0.5 / z))
        r2_ref[h] = pltpu.bitcast(rank2.astype(BF16), jnp.uint32)
        e2_ref[h] = pltpu.bitcast(jnp.exp(s2 - a2[0:1, :]).astype(BF16), jnp.uint32)


def _peer_route(x2, norm_g, wq, keys, tm):
    t = x2.shape[0]
    row_spec = pl.BlockSpec((N_HEADS, N_KEYS, tm), lambda i: (0, 0, i))
    pair_spec = pl.BlockSpec((N_HEADS, N_KEYS // 2, tm), lambda i: (0, 0, i))
    row_tab = jax.ShapeDtypeStruct((N_HEADS, N_KEYS, t), jnp.uint32)
    pair_tab = jax.ShapeDtypeStruct((N_HEADS, N_KEYS // 2, t), jnp.uint32)
    return pl.pallas_call(
        _peer_route_kernel,
        grid=(t // tm,),
        in_specs=[pl.BlockSpec((tm, D_MODEL), lambda i: (i, 0)),
                  pl.BlockSpec((1, D_MODEL), lambda i: (0, 0)),
                  pl.BlockSpec(wq.shape, lambda i: (0, 0)),
                  pl.BlockSpec(keys.shape, lambda i: (0, 0, 0))],
        out_specs=[pl.BlockSpec((D_MODEL, tm), lambda i: (0, i)), row_spec, row_spec, pair_spec, pair_spec],
        out_shape=[jax.ShapeDtypeStruct((D_MODEL, t), BF16), row_tab, row_tab, pair_tab, pair_tab],
        scratch_shapes=[pltpu.VMEM((A_ROWS, tm), F32) for _ in range(3)],
        compiler_params=_cparams(("parallel",)),
        name="peer_route",
    )(x2, norm_g.reshape(1, D_MODEL), wq, keys)


PEER_EC = SUBLANES * N_KEYS
PIPE_STAGES = 3
I_GROUP = 2


def _peer_expert_kernel(hnt_ref, u_ref, vt_ref, cnt_ref, e1z_ref, r2_ref, e2_ref, x_ref, out_ref,
                        acc_sc, s0_sc, s1_sc, p0_sc, p1_sc, *, ec):
    g = pl.program_id(1)
    tm = hnt_ref.shape[1]
    n_i = ec // N_KEYS

    def row_bf16(rows, k, ls):
        words = jnp.broadcast_to(rows[k:k + 1, ls], (N_KEYS // 2, LANES))
        return pltpu.bitcast(words, BF16)

    zero = jnp.zeros((), BF16)

    def gate_tile(ci, lt, s_in, p_out):
        ls = slice(lt * LANES, (lt + 1) * LANES)
        for i0 in range(0, n_i, I_GROUP):
            gates = [jnp.zeros((N_KEYS, LANES), BF16) for _ in range(I_GROUP)]
            for h in range(N_HEADS):
                r2t = pltpu.bitcast(r2_ref[h][:, ls], BF16)
                e2t = pltpu.bitcast(e2_ref[h][:, ls], BF16)
                tile0 = pl.multiple_of(ci * n_i + (i0 // SUBLANES) * SUBLANES, SUBLANES)
                cnt8 = cnt_ref[h, pl.ds(tile0, SUBLANES), :]
                e1z8 = e1z_ref[h, pl.ds(tile0, SUBLANES), :]
                for kk in range(I_GROUP):
                    k = i0 % SUBLANES + kk
                    sel = jnp.where(r2t < row_bf16(cnt8, k, ls), e2t, zero)
                    gates[kk] = gates[kk] + sel * row_bf16(e1z8, k, ls)
            for kk in range(I_GROUP):
                r0 = (i0 + kk) * N_KEYS
                s = s_in[r0:r0 + N_KEYS, ls].astype(BF16)
                act2 = s * (1.0 + lax.erf(s * (2.0 ** -0.5)))
                p_out[r0 // 2:(r0 + N_KEYS) // 2, ls] = pltpu.bitcast(gates[kk] * act2, jnp.uint32)

    def pipeline_step(c, u_blk, vt_blk, s_in, s_out, p_in, p_out, run_out, run_gate, run_pre):
        tiles_per_slab = MXU_LANES // LANES
        for sb in range(tm // MXU_LANES):
            cs = slice(sb * MXU_LANES, (sb + 1) * MXU_LANES)
            if run_out:
                acc_sc[:, cs] += jnp.dot(vt_blk, pltpu.bitcast(p_in[:, cs], BF16),
                                         preferred_element_type=F32)
            if run_gate:
                gate_tile(c - 1, sb * tiles_per_slab, s_in, p_out)
            if run_pre:
                s_out[:, cs] = jnp.dot(u_blk, hnt_ref[:, cs], preferred_element_type=F32)
            if run_gate:
                for lt in range(sb * tiles_per_slab + 1, (sb + 1) * tiles_per_slab):
                    gate_tile(c - 1, lt, s_in, p_out)

    def grid_step(first, last):
        even = (u_ref[:ec, :], vt_ref[:, :ec], s1_sc, s0_sc, p1_sc, p0_sc)
        odd = (u_ref[ec:, :], vt_ref[:, ec:], s0_sc, s1_sc, p0_sc, p1_sc)
        pipeline_step(2 * g, *even, run_out=not first, run_gate=not first, run_pre=not last)
        pipeline_step(2 * g + 1, *odd, run_out=not first, run_gate=not last, run_pre=not last)

    last_g = pl.num_programs(1) - 1

    @pl.when(g == 0)
    def _():
        acc_sc[...] = jnp.zeros_like(acc_sc)
        grid_step(True, False)

    @pl.when((g > 0) & (g < last_g))
    def _():
        grid_step(False, False)

    @pl.when(g == last_g)
    def _():
        grid_step(False, True)
        out_ref[...] = x_ref[...] + acc_sc[...].T


def _peer_expert(x2, hnt, u, vt, cnt, e1z, r2, e2, tm, ec):
    t = x2.shape[0]
    assert ec % (SUBLANES * N_KEYS) == 0 and (N_EXPERTS // ec) % 2 == 0
    n_chunks = N_EXPERTS // ec
    n_steps = (n_chunks + PIPE_STAGES - 1) // 2
    last_u = n_chunks // 2 - 1
    row_spec = pl.BlockSpec((N_HEADS, N_KEYS, tm), lambda i, g: (0, 0, i))
    pair_spec = pl.BlockSpec((N_HEADS, N_KEYS // 2, tm), lambda i, g: (0, 0, i))
    return pl.pallas_call(
        functools.partial(_peer_expert_kernel, ec=ec),
        grid=(t // tm, n_steps),
        in_specs=[pl.BlockSpec((D_MODEL, tm), lambda i, g: (0, i)),
                  pl.BlockSpec((2 * ec, D_MODEL), lambda i, g: (jnp.minimum(g, last_u), 0)),
                  pl.BlockSpec((D_MODEL, 2 * ec), lambda i, g: (0, jnp.maximum(g - 1, 0))),
                  row_spec, row_spec, pair_spec, pair_spec,
                  pl.BlockSpec((tm, D_MODEL), lambda i, g: (i, 0))],
        out_specs=pl.BlockSpec((tm, D_MODEL), lambda i, g: (i, 0)),
        out_shape=jax.ShapeDtypeStruct((t, D_MODEL), F32),
        scratch_shapes=[pltpu.VMEM((D_MODEL, tm), F32),
                        pltpu.VMEM((ec, tm), F32), pltpu.VMEM((ec, tm), F32),
                        pltpu.VMEM((ec // 2, tm), jnp.uint32), pltpu.VMEM((ec // 2, tm), jnp.uint32)],
        compiler_params=_cparams(("parallel", "arbitrary")),
        name="peer_expert",
    )(hnt, u, vt, cnt, e1z, r2, e2, x2)


def _peer_ffn(x2, norm_g, w_query, sub_keys, u_table, v_table, tm_route, tm_exp, ec):
    keys = sub_keys.reshape(2 * N_HEADS, N_KEYS, LANES).astype(BF16)
    hnt, cnt, e1z, r2, e2 = _peer_route(x2, norm_g, w_query.astype(BF16), keys, tm_route)
    return _peer_expert(x2, hnt, u_table.astype(BF16), v_table.T.astype(BF16), cnt, e1z, r2, e2, tm_exp, ec)


def kernel(x, positions, l0_norm_mix, l0_w_in, l0_b_gates, l0_head_norm, l0_w_out, l0_norm_ffn, l0_peer_wq, l0_peer_keys, l0_peer_u, l0_peer_v, l1_norm_mix, l1_w_qkv, l1_q_norm, l1_k_norm, l1_lambda_q1, l1_lambda_k1, l1_lambda_q2, l1_lambda_k2, l1_subln, l1_w_out, l1_norm_ffn, l1_peer_wq, l1_peer_keys, l1_peer_u, l1_peer_v):
    b, s, d = x.shape
    t = b * s
    tm = min(TOKEN_TILE, t)
    tm_route = min(ROUTE_TILE, t)
    x2 = x.reshape(t, d)

    w_k, w_rows = _mlstm_layout(l0_w_in)
    k, qt, vt, ot, gt = _mlstm_proj(x2, l0_norm_mix, w_k, w_rows, tm)
    hf, hb = _mlstm(k, qt, vt, gt, l0_b_gates, b, s)
    x2 = _mlstm_out(x2, hf, hb, ot, l0_head_norm, l0_w_out.astype(BF16), tm)
    x2 = _peer_ffn(x2, l0_norm_ffn, l0_peer_wq, l0_peer_keys, l0_peer_u, l0_peer_v, tm_route, tm, PEER_EC)

    lambda_init = 0.8 - 0.6 * math.exp(-0.3 * 1)
    pos = positions.reshape(t, 1).astype(F32)
    qa, ka, va = _da_proj(x2, l1_norm_mix, l1_w_qkv, l1_q_norm, l1_k_norm, pos, tm)
    to3 = lambda a: a.reshape(b, s, a.shape[-1])
    lam4 = jnp.stack([l1_lambda_q1, l1_lambda_k1, l1_lambda_q2, l1_lambda_k2]).astype(F32)
    att = _da_attn(to3(qa), ka, to3(va), lam4, l1_subln, b, s, lambda_init, DA_QUERY_TILE)
    x2 = _resid_matmul(x2, att.reshape(t, D_MODEL), l1_w_out.astype(BF16), tm)
    x2 = _peer_ffn(x2, l1_norm_ffn, l1_peer_wq, l1_peer_keys, l1_peer_u, l1_peer_v, tm_route, tm, PEER_EC)
    return x2.reshape(b, s, d)
```
